```python
import jax, jax.numpy as jnp
from jax import lax
import numpy as np

D_MODEL = 2048
BATCH = 1
SEQ = 8192
DEPTH = 1
DEC_BATCH = 4
DEC_SEQ = 2048
PAST_LEN = 128

MLA_HEADS = D_MODEL // 128
Q_LORA = 768
KV_LORA = 512
NOPE_DIM = 128
ROPE_DIM = 64
V_DIM = 128
QK_DIM = NOPE_DIM + ROPE_DIM
ROPE_THETA = 10000.0
Q_BLOCK = 128
D_MLA_V = MLA_HEADS * V_DIM

RWKV_HEAD = 64
D_RWKV = D_MODEL
RWKV_HEADS = D_RWKV // RWKV_HEAD
DECAY_LORA = 96
AAA_LORA = 96
GATE_LORA = 256

D_FF = 4 * D_MODEL
NORM_EPS = 1e-6
GN_EPS = 64e-5

MLA_IN = Q_LORA + KV_LORA + ROPE_DIM
RWKV_IN = 3 * D_RWKV + 2 * DECAY_LORA + 2 * AAA_LORA + GATE_LORA
N_IN = MLA_IN + RWKV_IN + 2 * D_MODEL
D_MIX = D_MLA_V + D_RWKV

kernel_name = 'hybrid_mla_rwkv7_gated_encoder'


def _rmsnorm(x, g):
    xf = x.astype(jnp.float32)
    y = xf * lax.rsqrt(jnp.mean(xf * xf, axis=-1, keepdims=True) + NORM_EPS)
    return (y * g.astype(jnp.float32)).astype(x.dtype)


def _rope_tables(seq):
    inv = 1.0 / (ROPE_THETA ** (jnp.arange(0, ROPE_DIM, 2, dtype=jnp.float32) / ROPE_DIM))
    ang = jnp.arange(seq, dtype=jnp.float32)[:, None] * inv[None, :]
    return jnp.cos(ang), jnp.sin(ang)


def _apply_rope(x, cos, sin):
    xf = x.astype(jnp.float32)
    x1, x2 = jnp.split(xf, 2, axis=-1)
    return jnp.concatenate([x1 * cos - x2 * sin, x1 * sin + x2 * cos], axis=-1).astype(x.dtype)


def _centred_shift(u, mu_prev, mu_next):
    prev = jnp.pad(u[:, :-1], ((0, 0), (1, 0), (0, 0)))
    nxt = jnp.pad(u[:, 1:], ((0, 0), (0, 1), (0, 0)))
    return u + mu_prev * (prev - u) + mu_next * (nxt - u)


def _mla(c_q, c_kv, k_rope, q_norm, w_uq, kv_norm, w_ukv):
    b, s, _ = c_q.shape
    q = (_rmsnorm(c_q, q_norm) @ w_uq).reshape(b, s, MLA_HEADS, QK_DIM)
    kv = (_rmsnorm(c_kv, kv_norm) @ w_ukv).reshape(b, s, MLA_HEADS, NOPE_DIM + V_DIM)
    q_nope, q_rope = q[..., :NOPE_DIM], q[..., NOPE_DIM:]
    k_nope, v = kv[..., :NOPE_DIM], kv[..., NOPE_DIM:]
    cos, sin = _rope_tables(s)
    q_rope = _apply_rope(q_rope, cos[None, :, None, :], sin[None, :, None, :])
    k_rope = _apply_rope(k_rope, cos[None], sin[None])
    scale = QK_DIM ** -0.5

    def attend_block(i):
        qn = lax.dynamic_slice_in_dim(q_nope, i * Q_BLOCK, Q_BLOCK, axis=1)
        qr = lax.dynamic_slice_in_dim(q_rope, i * Q_BLOCK, Q_BLOCK, axis=1)
        sc = (jnp.einsum('bqhd,bkhd->bhqk', qn, k_nope, preferred_element_type=jnp.float32)
              + jnp.einsum('bqhr,bkr->bhqk', qr, k_rope, preferred_element_type=jnp.float32))
        p = jax.nn.softmax(sc * scale, axis=-1).astype(v.dtype)
        return jnp.einsum('bhqk,bkhd->bqhd', p, v)

    out = lax.map(attend_block, jnp.arange(s // Q_BLOCK))
    return jnp.transpose(out, (1, 0, 2, 3, 4)).reshape(b, s, D_MLA_V)


def _wkv_scan(r, w, k, v, aa, bb, reverse):
    b = r.shape[0]

    def step(state, inp):
        r_t, w_t, k_t, v_t, a_t, b_t = inp
        sa = jnp.einsum('bhvk,bhk->bhv', state, a_t)
        state = (state * w_t[:, :, None, :] + sa[..., None] * b_t[:, :, None, :]
                 + v_t[..., None] * k_t[:, :, None, :])
        return state, jnp.einsum('bhvk,bhk->bhv', state, r_t)

    xs = tuple(jnp.moveaxis(t, 1, 0) for t in (r, w, k, v, aa, bb))
    s0 = jnp.zeros((b, RWKV_HEADS, RWKV_HEAD, RWKV_HEAD), jnp.float32)
    _, ys = lax.scan(step, s0, xs, reverse=reverse)
    return jnp.moveaxis(ys, 0, 1)


def _rwkv7(u, w0_f, w2_f, w0_b, w2_b, a0_f, a2_f, a0_b, a2_b, g2, k_k, k_a, r_k, ln_w, ln_b):
    dt = u.dtype
    b, s, _ = u.shape
    f = lambda t: t.astype(jnp.float32)
    u = f(u)
    cuts = np.cumsum([D_RWKV, D_RWKV, D_RWKV, DECAY_LORA, DECAY_LORA, AAA_LORA, AAA_LORA]).tolist()
    r, k, v, xw_f, xw_b, xa_f, xa_b, xg = jnp.split(u, cuts, axis=-1)
    heads = lambda t: t.reshape(b, s, RWKV_HEADS, RWKV_HEAD)

    def decay(xw, w0, w2):
        logw = -jax.nn.softplus(-(f(w0) + jnp.tanh(xw) @ f(w2))) - 0.5
        return heads(jnp.exp(-jnp.exp(logw)))

    a_f = jax.nn.sigmoid(f(a0_f) + xa_f @ f(a2_f))
    a_b = jax.nn.sigmoid(f(a0_b) + xa_b @ f(a2_b))
    g = jax.nn.sigmoid(xg) @ f(g2)
    kk = heads(k * f(k_k))
    kk = kk / jnp.maximum(jnp.linalg.norm(kk, axis=-1, keepdims=True), 1e-12)
    k_f = heads(k * (1.0 + (a_f - 1.0) * f(k_a)))
    k_b = heads(k * (1.0 + (a_b - 1.0) * f(k_a)))
    rh, vh = heads(r), heads(v)
    y = (_wkv_scan(rh, decay(xw_f, w0_f, w2_f), k_f, vh, -kk, kk * heads(a_f), False)
         + _wkv_scan(rh, decay(xw_b, w0_b, w2_b), k_b, vh, -kk, kk * heads(a_b), True))
    mean = jnp.mean(y, axis=-1, keepdims=True)
    var = jnp.mean(jnp.square(y - mean), axis=-1, keepdims=True)
    y = ((y - mean) * lax.rsqrt(var + GN_EPS)).reshape(b, s, D_RWKV) * f(ln_w) + f(ln_b)
    bonus = (jnp.sum(rh * (k_f + k_b) * f(r_k), axis=-1, keepdims=True) * vh).reshape(b, s, D_RWKV)
    return ((y + bonus) * g).astype(dt)


def _block(x, norm_pre_mix, w_in, mu_prev, mu_next, mla_q_norm, mla_w_uq, mla_kv_norm, mla_w_ukv,
           rwkv_w0_f, rwkv_w2_f, rwkv_w0_b, rwkv_w2_b, rwkv_a0_f, rwkv_a2_f, rwkv_a0_b, rwkv_a2_b,
           rwkv_g2, rwkv_k_k, rwkv_k_a, rwkv_r_k, rwkv_ln_w, rwkv_ln_b, w_branch, w_out,
           norm_post_mix, norm_pre_mlp, w_mlp_up, w_mlp_down, norm_post_mlp):
    h = _rmsnorm(x, norm_pre_mix)
    proj = h @ w_in
    c_q = proj[..., :Q_LORA]
    c_kv = proj[..., Q_LORA:Q_LORA + KV_LORA]
    k_rope = proj[..., Q_LORA + KV_LORA:MLA_IN]
    u = proj[..., MLA_IN:MLA_IN + RWKV_IN]
    gates = jax.nn.sigmoid(proj[..., MLA_IN + RWKV_IN:])
    y_a = _mla(c_q, c_kv, k_rope, mla_q_norm, mla_w_uq, mla_kv_norm, mla_w_ukv)
    y_b = _rwkv7(_centred_shift(u, mu_prev, mu_next), rwkv_w0_f, rwkv_w2_f, rwkv_w0_b, rwkv_w2_b,
                 rwkv_a0_f, rwkv_a2_f, rwkv_a0_b, rwkv_a2_b, rwkv_g2, rwkv_k_k, rwkv_k_a,
                 rwkv_r_k, rwkv_ln_w, rwkv_ln_b)
    o_a = y_a @ w_branch[:D_MLA_V]
    o_b = y_b @ w_branch[D_MLA_V:]
    mix = (gates[..., :D_MODEL] * o_a + gates[..., D_MODEL:] * o_b) @ w_out
    x = x + _rmsnorm(mix, norm_post_mix)
    ff = jnp.square(jax.nn.relu(_rmsnorm(x, norm_pre_mlp) @ w_mlp_up)) @ w_mlp_down
    return x + _rmsnorm(ff, norm_post_mlp)


def setup_inputs(seed: int = 0) -> dict:
    key = jax.random.key(seed)
    ks = iter(jax.random.split(key, 40))
    L = DEPTH
    nrm = lambda shape, scale: scale * jax.random.normal(next(ks), shape, jnp.float32)
    gain = lambda n: 1.0 + nrm((L, n), 0.05)
    return {
        'x_prompt': nrm((BATCH, SEQ, D_MODEL), 1.0),
        'x_sample': nrm((DEC_BATCH, DEC_SEQ, D_MODEL), 1.0),
        'norm_pre_mix': gain(D_MODEL),
        'w_in': nrm((L, D_MODEL, N_IN), D_MODEL ** -0.5),
        'mu_prev': jax.random.uniform(next(ks), (L, RWKV_IN), jnp.float32, 0.0, 0.5),
        'mu_next': jax.random.uniform(next(ks), (L, RWKV_IN), jnp.float32, 0.0, 0.5),
        'mla_q_norm': gain(Q_LORA),
        'mla_w_uq': nrm((L, Q_LORA, MLA_HEADS * QK_DIM), Q_LORA ** -0.5),
        'mla_kv_norm': gain(KV_LORA),
        'mla_w_ukv': nrm((L, KV_LORA, MLA_HEADS * (NOPE_DIM + V_DIM)), KV_LORA ** -0.5),
        'rwkv_w0_f': nrm((L, D_RWKV), 0.5),
        'rwkv_w2_f': nrm((L, DECAY_LORA, D_RWKV), 0.1 * DECAY_LORA ** -0.5),
        'rwkv_w0_b': nrm((L, D_RWKV), 0.5),
        'rwkv_w2_b': nrm((L, DECAY_LORA, D_RWKV), 0.1 * DECAY_LORA ** -0.5),
        'rwkv_a0_f': nrm((L, D_RWKV), 0.5),
        'rwkv_a2_f': nrm((L, AAA_LORA, D_RWKV), 0.3 * AAA_LORA ** -0.5),
        'rwkv_a0_b': nrm((L, D_RWKV), 0.5),
        'rwkv_a2_b': nrm((L, AAA_LORA, D_RWKV), 0.3 * AAA_LORA ** -0.5),
        'rwkv_g2': nrm((L, GATE_LORA, D_RWKV), GATE_LORA ** -0.5),
        'rwkv_k_k': 1.0 + nrm((L, D_RWKV), 0.1),
        'rwkv_k_a': 1.0 + nrm((L, D_RWKV), 0.1),
        'rwkv_r_k': nrm((L, RWKV_HEADS, RWKV_HEAD), 0.1),
        'rwkv_ln_w': gain(D_RWKV),
        'rwkv_ln_b': nrm((L, D_RWKV), 0.01),
        'w_branch': nrm((L, D_MIX, D_MODEL), D_MODEL ** -0.5),
        'w_out': nrm((L, D_MODEL, D_MODEL), D_MODEL ** -0.5),
        'norm_post_mix': gain(D_MODEL),
        'norm_pre_mlp': gain(D_MODEL),
        'w_mlp_up': nrm((L, D_MODEL, D_FF), D_MODEL ** -0.5),
        'w_mlp_down': nrm((L, D_FF, D_MODEL), D_FF ** -0.5),
        'norm_post_mlp': gain(D_MODEL),
    }


def reference(x_prompt, x_sample, norm_pre_mix, w_in, mu_prev, mu_next, mla_q_norm, mla_w_uq,
              mla_kv_norm, mla_w_ukv, rwkv_w0_f, rwkv_w2_f, rwkv_w0_b, rwkv_w2_b, rwkv_a0_f,
              rwkv_a2_f, rwkv_a0_b, rwkv_a2_b, rwkv_g2, rwkv_k_k, rwkv_k_a, rwkv_r_k, rwkv_ln_w,
              rwkv_ln_b, w_branch, w_out, norm_post_mix, norm_pre_mlp, w_mlp_up, w_mlp_down,
              norm_post_mlp):
    params = (norm_pre_mix, w_in, mu_prev, mu_next, mla_q_norm, mla_w_uq, mla_kv_norm, mla_w_ukv,
              rwkv_w0_f, rwkv_w2_f, rwkv_w0_b, rwkv_w2_b, rwkv_a0_f, rwkv_a2_f, rwkv_a0_b,
              rwkv_a2_b, rwkv_g2, rwkv_k_k, rwkv_k_a, rwkv_r_k, rwkv_ln_w, rwkv_ln_b, w_branch,
              w_out, norm_post_mix, norm_pre_mlp, w_mlp_up, w_mlp_down, norm_post_mlp)
    y_prompt = x_prompt
    y_sample = x_sample
    for layer in range(DEPTH):
        lp = tuple(p[layer] for p in params)
        y_prompt = _block(y_prompt, *lp)
        y_sample = _block(y_sample, *lp)
    return (y_prompt, y_sample)
```

```python
import functools

import numpy as np
import jax
import jax.numpy as jnp
from jax import lax
from jax.experimental import pallas as pl
from jax.experimental.pallas import tpu as pltpu

f32 = jnp.float32
bf16 = jnp.bfloat16

D_MODEL = 2048
T_PROMPT = 8192
S_SAMPLE = 2048
B_SAMPLE = 4
T_ALL = T_PROMPT + B_SAMPLE * S_SAMPLE

MLA_HEADS = 16
Q_LORA = 768
KV_LORA = 512
NOPE = 128
ROPE = 64
V_DIM = 128
QK_DIM = NOPE + ROPE
ROPE_THETA = 10000.0

RWKV_HEAD = 64
RWKV_HEADS = 32
LORA_PAD = 128
GATE_LORA = 256
U_PACKED = 3 * D_MODEL + 4 * LORA_PAD + GATE_LORA

D_FF = 4 * D_MODEL
NORM_EPS = 1e-6
GN_EPS = 64e-5

CHUNK = 64
GROUP = 256
VMEM_LIMIT = 56 * 1024 * 1024


def _rms(x, g):
    return x * lax.rsqrt(jnp.mean(x * x, axis=-1, keepdims=True) + NORM_EPS) * g


def _dot(a, b):
    return jnp.dot(a, b, preferred_element_type=f32)


def _dot_nt(a, b):
    return lax.dot_general(a, b, (((1,), (1,)), ((), ())), preferred_element_type=f32)


def _dot_tn(a, b):
    return lax.dot_general(a, b, (((0,), (0,)), ((), ())), preferred_element_type=f32)


def _split_dot(x, w):
    hi = x.astype(bf16)
    lo = (x - hi.astype(f32)).astype(bf16)
    return _dot(hi, w) + _dot(lo, w)


def _const_spec(shape):
    return pl.BlockSpec(shape, lambda *_: (0,) * len(shape), pipeline_mode=pl.Buffered(1))


def _mla_front_kernel(x_ref, g_ref, wm_ref, qg_ref, wq_ref, kvg_ref, wkv_ref, cos_ref, sin_ref,
                      qn_ref, qr_ref, kn_ref, v_ref, kr_ref):
    h = _rms(x_ref[...], g_ref[...]).astype(bf16)
    pm = _dot(h, wm_ref[...])
    cos = cos_ref[...]
    sin = sin_ref[...]
    kr_ref[...] = (pm[:, 1280:1408] * cos + pm[:, 1408:1536] * sin).astype(bf16)
    qin = _rms(pm[:, :Q_LORA], qg_ref[...]).astype(bf16)
    kvin = _rms(pm[:, Q_LORA:Q_LORA + KV_LORA], kvg_ref[...]).astype(bf16)
    scale = QK_DIM ** -0.5
    cw = 512
    for c in range(D_MODEL // cw):
        q = _dot(qin, wq_ref[:, c * cw:(c + 1) * cw])
        qn_ref[:, c * cw:(c + 1) * cw] = (q * scale).astype(bf16)
    cos4 = jnp.concatenate([cos] * (cw // 128), axis=1)
    sin4 = jnp.concatenate([sin] * (cw // 128), axis=1)
    nrope = MLA_HEADS * ROPE
    for c in range(nrope // cw):
        qa = _dot(qin, wq_ref[:, D_MODEL + c * cw:D_MODEL + (c + 1) * cw])
        qb = _dot(qin, wq_ref[:, D_MODEL + nrope + c * cw:D_MODEL + nrope + (c + 1) * cw])
        qr_ref[:, c * cw:(c + 1) * cw] = ((qa * cos4 + qb * sin4) * scale).astype(bf16)
    for c in range(D_MODEL // cw):
        kn_ref[:, c * cw:(c + 1) * cw] = _dot(kvin, wkv_ref[:, c * cw:(c + 1) * cw]).astype(bf16)
        v_ref[:, c * cw:(c + 1) * cw] = _dot(
            kvin, wkv_ref[:, D_MODEL + c * cw:D_MODEL + (c + 1) * cw]).astype(bf16)


def _mla_front(x, g, w_mla, qg, w_q, kvg, w_kv, cos, sin, tm=512):
    nq = w_q.shape[1]
    row = lambda w: pl.BlockSpec((tm, w), lambda i: (i, 0))
    return pl.pallas_call(
        _mla_front_kernel,
        grid=(T_ALL // tm,),
        in_specs=[row(D_MODEL), _const_spec((1, D_MODEL)), _const_spec(w_mla.shape),
                  _const_spec((1, Q_LORA)), _const_spec((Q_LORA, nq)),
                  _const_spec((1, KV_LORA)), _const_spec(w_kv.shape), row(128), row(128)],
        out_specs=[row(D_MODEL), row(MLA_HEADS * ROPE), row(D_MODEL), row(D_MODEL), row(128)],
        out_shape=[jax.ShapeDtypeStruct((T_ALL, D_MODEL), bf16),
                   jax.ShapeDtypeStruct((T_ALL, MLA_HEADS * ROPE), bf16),
                   jax.ShapeDtypeStruct((T_ALL, D_MODEL), bf16),
                   jax.ShapeDtypeStruct((T_ALL, D_MODEL), bf16),
                   jax.ShapeDtypeStruct((T_ALL, 128), bf16)],
        compiler_params=pltpu.CompilerParams(dimension_semantics=("parallel",),
                                             vmem_limit_bytes=VMEM_LIMIT),
        name="mla_front",
    )(x, g, w_mla, qg, w_q, kvg, w_kv, cos, sin)


def _norm_matmul_kernel(x_ref, g_ref, w_ref, o_ref, h_ref, *, sigmoid):
    @pl.when(pl.program_id(1) == 0)
    def _():
        h_ref[...] = _rms(x_ref[...], g_ref[...]).astype(bf16)

    acc = _dot(h_ref[...], w_ref[...])
    if sigmoid:
        acc = jax.nn.sigmoid(acc)
    o_ref[...] = acc.astype(o_ref.dtype)


def _norm_matmul(x, g, w, tn, out_dtype, sigmoid, name, tm=1024):
    n = w.shape[1]
    return pl.pallas_call(
        functools.partial(_norm_matmul_kernel, sigmoid=sigmoid),
        grid=(T_ALL // tm, n // tn),
        in_specs=[pl.BlockSpec((tm, D_MODEL), lambda i, j: (i, 0)),
                  _const_spec((1, D_MODEL)),
                  pl.BlockSpec((D_MODEL, tn), lambda i, j: (0, j))],
        out_specs=pl.BlockSpec((tm, tn), lambda i, j: (i, j)),
        out_shape=jax.ShapeDtypeStruct((T_ALL, n), out_dtype),
        scratch_shapes=[pltpu.VMEM((tm, D_MODEL), bf16)],
        compiler_params=pltpu.CompilerParams(dimension_semantics=("parallel", "arbitrary"),
                                             vmem_limit_bytes=VMEM_LIMIT),
        name=name,
    )(x, g, w)


def _attn_kernel(qn_ref, qr_ref, kn_ref, kr_ref, v_ref, o_ref, kcat_ref, *, seq, tq, tk):
    head = pl.program_id(1)

    @pl.when(pl.program_id(2) == 0)
    def _():
        kcat_ref[:, :NOPE] = kn_ref[...]
        kcat_ref[:, NOPE:] = kr_ref[...]

    lane = lax.broadcasted_iota(jnp.int32, (tq, 128), 1)
    keep = (lane // ROPE) == (head % 2)
    qr = jnp.where(keep, qr_ref[...].astype(f32), 0.0).astype(bf16)
    q = jnp.concatenate([qn_ref[...], qr], axis=1)

    def body(j, carry):
        m, l, acc = carry
        rows = pl.ds(pl.multiple_of(j * tk, tk), tk)
        s = _dot_nt(q, kcat_ref[rows, :])
        m_new = jnp.maximum(m, jnp.max(s, axis=1, keepdims=True))
        alpha = jnp.exp(m - m_new)
        p = jnp.exp(s - m_new)
        l = alpha * l + jnp.sum(p, axis=1, keepdims=True)
        acc = alpha * acc + _dot(p.astype(bf16), v_ref[rows, :])
        return m_new, l, acc

    m0 = jnp.full((tq, 1), -jnp.inf, f32)
    l0 = jnp.zeros((tq, 1), f32)
    a0 = jnp.zeros((tq, V_DIM), f32)
    _, l, acc = lax.fori_loop(0, seq // tk, body, (m0, l0, a0))
    o_ref[...] = (acc / l).astype(o_ref.dtype)


def _attention(qn, qr, kn, kr, v, row_off, seq, nb, tq=256, tk=512):
    qoff = row_off // tq
    soff = row_off // seq
    nq = seq // tq
    return pl.pallas_call(
        functools.partial(_attn_kernel, seq=seq, tq=tq, tk=tk),
        grid=(nb, MLA_HEADS, nq),
        in_specs=[pl.BlockSpec((tq, NOPE), lambda b, h, i: (qoff + b * nq + i, h)),
                  pl.BlockSpec((tq, 128), lambda b, h, i: (qoff + b * nq + i, h // 2)),
                  pl.BlockSpec((seq, NOPE), lambda b, h, i: (soff + b, h)),
                  pl.BlockSpec((seq, 128), lambda b, h, i: (soff + b, 0)),
                  pl.BlockSpec((seq, V_DIM), lambda b, h, i: (soff + b, h))],
        out_specs=pl.BlockSpec((tq, V_DIM), lambda b, h, i: (b * nq + i, h)),
        out_shape=jax.ShapeDtypeStruct((nb * seq, D_MODEL), bf16),
        scratch_shapes=[pltpu.VMEM((seq, 2 * NOPE), bf16)],
        compiler_params=pltpu.CompilerParams(
            dimension_semantics=("parallel", "parallel", "arbitrary"),
            vmem_limit_bytes=VMEM_LIMIT),
        name=f"mla_attention_s{seq}",
    )(qn, qr, kn, kr, v)


def _seg_sum(x, bd):
    cols = []
    for c in range(x.shape[1] // GROUP):
        cols.append(_split_dot(x[:, c * GROUP:(c + 1) * GROUP], bd))
    return jnp.concatenate(cols, axis=1)


def _rwkv_prep_kernel(u_ref, up_ref, un_ref, mup_ref, mun_ref,
                      w0f_ref, w2f_ref, w0b_ref, w2b_ref, a0f_ref, a2f_ref, a0b_ref, a2b_ref,
                      g2_ref, kkw_ref, kaw_ref, rk_ref, bd_ref,
                      r_ref, v_ref, kk_ref, kf_ref, kb_ref, bf_ref, bb_ref, lwf_ref, lwb_ref,
                      g_ref, bonus_ref, *, tm):
    t0 = pl.program_id(0) * tm
    t1 = t0 + tm
    first = jnp.logical_or(t0 == 0, jnp.logical_and(t0 >= T_PROMPT, (t0 - T_PROMPT) % S_SAMPLE == 0))
    last = jnp.logical_or(t1 == T_PROMPT, jnp.logical_and(t1 > T_PROMPT, (t1 - T_PROMPT) % S_SAMPLE == 0))
    keep_prev = jnp.where(first, 0.0, 1.0)
    keep_next = jnp.where(last, 0.0, 1.0)
    row = lax.broadcasted_iota(jnp.int32, (tm, 1), 0)

    def shifted(lo, hi):
        u = u_ref[:, lo:hi]
        prev_row = up_ref[7:8, lo:hi] * keep_prev
        next_row = un_ref[0:1, lo:hi] * keep_next
        prev = jnp.where(row == 0, prev_row, pltpu.roll(u, 1, axis=0))
        nxt = jnp.where(row == tm - 1, next_row, pltpu.roll(u, tm - 1, axis=0))
        return u + mup_ref[:, lo:hi] * (prev - u) + mun_ref[:, lo:hi] * (nxt - u)

    d = D_MODEL
    r = shifted(0, d)
    k = shifted(d, 2 * d)
    v = shifted(2 * d, 3 * d)
    small = shifted(3 * d, U_PACKED)
    xw_f = small[:, 0:128]
    xw_b = small[:, 128:256]
    xa_f = small[:, 256:384]
    xa_b = small[:, 384:512]
    xg = small[:, 512:768]

    def neg_exp_logw(xw, w0_ref, w2_ref):
        z = w0_ref[...] + _dot(jnp.tanh(xw).astype(bf16), w2_ref[...])
        logw = -jax.nn.softplus(-z) - 0.5
        return -jnp.exp(logw)

    lwf_ref[...] = neg_exp_logw(xw_f, w0f_ref, w2f_ref)
    lwb_ref[...] = neg_exp_logw(xw_b, w0b_ref, w2b_ref)
    a_f = jax.nn.sigmoid(a0f_ref[...] + _dot(xa_f.astype(bf16), a2f_ref[...]))
    a_b = jax.nn.sigmoid(a0b_ref[...] + _dot(xa_b.astype(bf16), a2b_ref[...]))
    g_ref[...] = _dot(jax.nn.sigmoid(xg).astype(bf16), g2_ref[...]).astype(bf16)

    bd = bd_ref[...]
    kk = k * kkw_ref[...]
    norm = jnp.sqrt(_seg_sum(kk * kk, bd))
    kk = kk / jnp.maximum(norm, 1e-12)
    ka = kaw_ref[...]
    k_f = k * (1.0 + (a_f - 1.0) * ka)
    k_b = k * (1.0 + (a_b - 1.0) * ka)
    bonus = _seg_sum(r * (k_f + k_b) * rk_ref[...], bd) * v

    r_ref[...] = r.astype(bf16)
    v_ref[...] = v.astype(bf16)
    kk_ref[...] = kk.astype(bf16)
    kf_ref[...] = k_f.astype(bf16)
    kb_ref[...] = k_b.astype(bf16)
    bf_ref[...] = (kk * a_f).astype(bf16)
    bb_ref[...] = (kk * a_b).astype(bf16)
    bonus_ref[...] = bonus.astype(bf16)


def _rwkv_prep(u, mup, mun, vecs, loras, g2, bd, tm=128):
    hb = tm // 8
    nblk8 = T_ALL // 8
    vec = _const_spec((1, D_MODEL))
    lora = _const_spec((LORA_PAD, D_MODEL))
    w0f, w0b, a0f, a0b, kkw, kaw, rk = vecs
    w2f, w2b, a2f, a2b = loras
    out_bf = jax.ShapeDtypeStruct((T_ALL, D_MODEL), bf16)
    out_f = jax.ShapeDtypeStruct((T_ALL, D_MODEL), f32)
    row = pl.BlockSpec((tm, D_MODEL), lambda i: (i, 0))
    return pl.pallas_call(
        functools.partial(_rwkv_prep_kernel, tm=tm),
        grid=(T_ALL // tm,),
        in_specs=[pl.BlockSpec((tm, U_PACKED), lambda i: (i, 0)),
                  pl.BlockSpec((8, U_PACKED), lambda i: (jnp.maximum(i * hb - 1, 0), 0)),
                  pl.BlockSpec((8, U_PACKED), lambda i: (jnp.minimum((i + 1) * hb, nblk8 - 1), 0)),
                  _const_spec((1, U_PACKED)), _const_spec((1, U_PACKED)),
                  vec, lora, vec, lora, vec, lora, vec, lora,
                  _const_spec((GATE_LORA, D_MODEL)), vec, vec, vec, _const_spec((GROUP, GROUP))],
        out_specs=[row] * 11,
        out_shape=[out_bf] * 7 + [out_f, out_f, out_bf, out_bf],
        compiler_params=pltpu.CompilerParams(dimension_semantics=("parallel",),
                                             vmem_limit_bytes=VMEM_LIMIT),
        name="rwkv_prep",
    )(u, u, u, mup, mun, w0f, w2f, w0b, w2b, a0f, a2f, a0b, a2b, g2, kkw, kaw, rk, bd)


def _wkv_kernel(r_ref, k_ref, v_ref, kk_ref, b_ref, lw_ref, y_ref, h_ref, *, reverse, cb):
    C = CHUNK

    @pl.when(pl.program_id(2) == 0)
    def _():
        h_ref[...] = jnp.zeros_like(h_ref)

    t_i = lax.broadcasted_iota(jnp.int32, (C, GROUP), 0)
    s_i = lax.broadcasted_iota(jnp.int32, (C, GROUP), 1) % C
    if reverse:
        strict = s_i > t_i
        incl = s_i >= t_i
    else:
        strict = s_i < t_i
        incl = s_i <= t_i
    eye = jnp.where(s_i == t_i, 1.0, 0.0)
    tt = lax.broadcasted_iota(jnp.int32, (C, C), 0)
    ss = lax.broadcasted_iota(jnp.int32, (C, C), 1)
    tri = jnp.where((ss >= tt) if reverse else (ss <= tt), 1.0, 0.0).astype(bf16)
    br = lax.broadcasted_iota(jnp.int32, (GROUP, GROUP), 0) // RWKV_HEAD
    bc = lax.broadcasted_iota(jnp.int32, (GROUP, GROUP), 1) // RWKV_HEAD
    bdmask = br == bc
    last = 0 if reverse else C - 1

    def bd(x):
        return jnp.where(bdmask, jnp.concatenate([x] * 4, axis=0), 0.0).astype(bf16)

    def ss_mul(x, y_bd):
        return _dot(x.astype(bf16), y_bd)

    def chunk(j, carry):
        jj = (cb - 1 - j) if reverse else j
        rows = pl.ds(pl.multiple_of(jj * C, C), C)
        r = r_ref[rows, :].astype(f32)
        k = k_ref[rows, :].astype(f32)
        v = v_ref[rows, :].astype(f32)
        kk = kk_ref[rows, :].astype(f32)
        b = b_ref[rows, :].astype(f32)
        lw = lw_ref[rows, :]

        lw_hi = lw.astype(bf16)
        lw_lo = (lw - lw_hi.astype(f32)).astype(bf16)
        cum = _dot(tri, lw_hi) + _dot(tri, lw_lo)
        tot = cum[last:last + 1, :]
        w_inc = jnp.exp(cum)
        w_inv = jnp.exp(-cum)
        w_ex = jnp.exp(cum - lw)
        w_rem = jnp.exp(tot - cum)
        a_t = -kk * w_ex
        r_t = r * w_inc
        b_t = b * w_inv
        k_t = k * w_inv
        b_h = b * w_rem
        k_h = k * w_rem

        lhs = jnp.concatenate([a_t, r_t], axis=0).astype(bf16)
        ab = _dot_nt(lhs, bd(b_t))
        ak = _dot_nt(lhs, bd(k_t))
        a_ab = jnp.where(strict, ab[:C], 0.0)
        a_rb = jnp.where(incl, ab[C:], 0.0)
        a_ak = jnp.where(strict, ak[:C], 0.0)
        a_rk = jnp.where(incl, ak[C:], 0.0)

        tinv = eye + a_ab
        pw = a_ab
        for _ in range(5):
            pw = ss_mul(pw, bd(pw))
            tinv = tinv + ss_mul(tinv, bd(pw))

        v_bd = bd(v)
        z = ss_mul(a_ak, v_bd)
        p = ss_mul(tinv, bd(a_t))
        q = ss_mul(tinv, bd(z))
        y_op = r_t + ss_mul(a_rb, bd(p))
        y_c = ss_mul(a_rb, bd(q)) + ss_mul(a_rk, v_bd)

        h = h_ref[...]
        uy = _dot(jnp.concatenate([p, y_op], axis=0).astype(bf16), h.astype(bf16))
        u = uy[:C] + q
        y_ref[rows, :] = uy[C:] + y_c
        upd = _dot_tn(jnp.concatenate([b_h, k_h], axis=0).astype(bf16),
                      jnp.concatenate([u, v], axis=0).astype(bf16))
        w_tot = jnp.exp(jnp.transpose(cum)[:, last:last + 1])
        h_ref[...] = h * w_tot + jnp.where(bdmask, upd, 0.0)
        return carry

    lax.fori_loop(0, cb, chunk, 0)


def _wkv_scan(r, k, v, kk, b, lw, row_off, seq, nb, reverse, cb=8):
    rb = cb * CHUNK
    nblk = seq // rb
    boff = row_off // rb

    def idx(s, g, c):
        cc = (nblk - 1 - c) if reverse else c
        return (boff + s * nblk + cc, g)

    spec = pl.BlockSpec((rb, GROUP), idx)
    return pl.pallas_call(
        functools.partial(_wkv_kernel, reverse=reverse, cb=cb),
        grid=(nb, D_MODEL // GROUP, nblk),
        in_specs=[spec] * 6,
        out_specs=pl.BlockSpec((rb, GROUP), lambda s, g, c: (s * nblk + ((nblk - 1 - c) if reverse else c), g)),
        out_shape=jax.ShapeDtypeStruct((nb * seq, D_MODEL), f32),
        scratch_shapes=[pltpu.VMEM((GROUP, GROUP), f32)],
        compiler_params=pltpu.CompilerParams(
            dimension_semantics=("parallel", "parallel", "arbitrary"),
            vmem_limit_bytes=VMEM_LIMIT),
        name=f"wkv_{'bwd' if reverse else 'fwd'}_s{seq}",
    )(r, k, v, kk, b, lw)


def _branch_kernel(ya_ref, yf_ref, yb_ref, bonus_ref, g_ref, g1_ref, g2_ref, lnw_ref, lnb_ref,
                   bd_ref, wa_ref, wb_ref, o_ref):
    bd = bd_ref[...]
    y = yf_ref[...] + yb_ref[...]
    inv_n = 1.0 / RWKV_HEAD
    mean = _seg_sum(y, bd) * inv_n
    dlt = y - mean
    var = _seg_sum(dlt * dlt, bd) * inv_n
    yn = dlt * lax.rsqrt(var + GN_EPS) * lnw_ref[...] + lnb_ref[...]
    yb = ((yn + bonus_ref[...].astype(f32)) * g_ref[...].astype(f32)).astype(bf16)
    ya = ya_ref[...]
    cw = 512
    for c in range(D_MODEL // cw):
        cs = slice(c * cw, (c + 1) * cw)
        oa = _dot(ya, wa_ref[:, cs])
        ob = _dot(yb, wb_ref[:, cs])
        o_ref[:, cs] = (g1_ref[:, cs].astype(f32) * oa + g2_ref[:, cs].astype(f32) * ob).astype(bf16)


def _branch(ya, yf, yb, bonus, g, gates, lnw, lnb, bd, wa, wb, tm=256):
    row = pl.BlockSpec((tm, D_MODEL), lambda i: (i, 0))
    return pl.pallas_call(
        _branch_kernel,
        grid=(T_ALL // tm,),
        in_specs=[row, row, row, row, row,
                  pl.BlockSpec((tm, D_MODEL), lambda i: (i, 0)),
                  pl.BlockSpec((tm, D_MODEL), lambda i: (i, 1)),
                  _const_spec((1, D_MODEL)), _const_spec((1, D_MODEL)), _const_spec((GROUP, GROUP)),
                  _const_spec((D_MODEL, D_MODEL)), _const_spec((D_MODEL, D_MODEL))],
        out_specs=row,
        out_shape=jax.ShapeDtypeStruct((T_ALL, D_MODEL), bf16),
        compiler_params=pltpu.CompilerParams(dimension_semantics=("parallel",),
                                             vmem_limit_bytes=VMEM_LIMIT),
        name="branch_mix",
    )(ya, yf, yb, bonus, g, gates, gates, lnw, lnb, bd, wa, wb)


def _out_proj_kernel(o_ref, x_ref, w_ref, g_ref, x1_ref):
    mix = _dot(o_ref[...], w_ref[...])
    x1_ref[...] = x_ref[...] + _rms(mix, g_ref[...])


def _out_proj(o, x, w, g, tm=512):
    row = pl.BlockSpec((tm, D_MODEL), lambda i: (i, 0))
    return pl.pallas_call(
        _out_proj_kernel,
        grid=(T_ALL // tm,),
        in_specs=[row, row, _const_spec((D_MODEL, D_MODEL)), _const_spec((1, D_MODEL))],
        out_specs=row,
        out_shape=jax.ShapeDtypeStruct((T_ALL, D_MODEL), f32),
        compiler_params=pltpu.CompilerParams(dimension_semantics=("parallel",),
                                             vmem_limit_bytes=VMEM_LIMIT),
        name="out_proj",
    )(o, x, w, g)


def _mlp_kernel(x_ref, gpre_ref, wu_ref, wd_ref, gpost_ref, o_ref, xn_ref, acc_ref):
    j = pl.program_id(1)

    @pl.when(j == 0)
    def _():
        xn_ref[...] = _rms(x_ref[...], gpre_ref[...]).astype(bf16)
        acc_ref[...] = jnp.zeros_like(acc_ref)

    h = jnp.maximum(_dot(xn_ref[...], wu_ref[...]), 0.0)
    acc_ref[...] += _dot((h * h).astype(bf16), wd_ref[...])

    @pl.when(j == pl.num_programs(1) - 1)
    def _():
        o_ref[...] = x_ref[...] + _rms(acc_ref[...], gpost_ref[...])


def _mlp(x1, gpre, w_up, w_down, gpost, tm=512, tf=1024):
    row = pl.BlockSpec((tm, D_MODEL), lambda i, j: (i, 0))
    return pl.pallas_call(
        _mlp_kernel,
        grid=(T_ALL // tm, D_FF // tf),
        in_specs=[row, _const_spec((1, D_MODEL)),
                  pl.BlockSpec((D_MODEL, tf), lambda i, j: (0, j)),
                  pl.BlockSpec((tf, D_MODEL), lambda i, j: (j, 0)),
                  _const_spec((1, D_MODEL))],
        out_specs=row,
        out_shape=jax.ShapeDtypeStruct((T_ALL, D_MODEL), f32),
        scratch_shapes=[pltpu.VMEM((tm, D_MODEL), bf16), pltpu.VMEM((tm, D_MODEL), f32)],
        compiler_params=pltpu.CompilerParams(dimension_semantics=("parallel", "arbitrary"),
                                             vmem_limit_bytes=VMEM_LIMIT),
        name="mlp",
    )(x1, gpre, w_up, w_down, gpost)


def _pad_lanes(a, width):
    return jnp.pad(a, [(0, 0)] * (a.ndim - 1) + [(0, width - a.shape[-1])])


def _pack_u(a):
    d = D_MODEL
    cuts = [3 * d, 3 * d + 96, 3 * d + 192, 3 * d + 288, 3 * d + 384]
    return jnp.concatenate(
        [a[..., :cuts[0]]]
        + [_pad_lanes(a[..., cuts[i]:cuts[i + 1]], LORA_PAD) for i in range(4)]
        + [a[..., cuts[4]:]], axis=-1)


def _pad_rows(w):
    return jnp.pad(w, [(0, LORA_PAD - w.shape[0]), (0, 0)])


def _rope_tables():
    inv = 1.0 / (ROPE_THETA ** (jnp.arange(0, ROPE, 2, dtype=f32) / ROPE))
    pos = jnp.concatenate([jnp.arange(T_PROMPT, dtype=f32),
                           jnp.tile(jnp.arange(S_SAMPLE, dtype=f32), B_SAMPLE)])
    ang = pos[:, None] * inv[None, :]
    cos, sin = jnp.cos(ang), jnp.sin(ang)
    cos = jnp.concatenate([cos, cos] * 2, axis=1)
    sin = jnp.concatenate([-sin, sin] * 2, axis=1)
    return cos, sin


def kernel(x_prompt, x_sample, norm_pre_mix, w_in, mu_prev, mu_next, mla_q_norm, mla_w_uq, mla_kv_norm, mla_w_ukv, rwkv_w0_f, rwkv_w2_f, rwkv_w0_b, rwkv_w2_b, rwkv_a0_f, rwkv_a2_f, rwkv_a0_b, rwkv_a2_b, rwkv_g2, rwkv_k_k, rwkv_k_a, rwkv_r_k, rwkv_ln_w, rwkv_ln_b, w_branch, w_out, norm_post_mix, norm_pre_mlp, w_mlp_up, w_mlp_down, norm_post_mlp):
    x = jnp.concatenate([x_prompt.reshape(T_PROMPT, D_MODEL),
                         x_sample.reshape(B_SAMPLE * S_SAMPLE, D_MODEL)], axis=0)

    swap = np.concatenate([np.arange(ROPE // 2, ROPE), np.arange(ROPE // 2)])
    w = w_in[0]
    mla_in = Q_LORA + KV_LORA + ROPE
    rwkv_in = 3 * D_MODEL + 4 * 96 + GATE_LORA
    w_kr = w[:, Q_LORA + KV_LORA:mla_in]
    w_krs = w_kr[:, swap]
    w_mla = jnp.concatenate([w[:, :Q_LORA + KV_LORA], w_kr, w_kr, w_krs, w_krs], axis=1).astype(bf16)
    w_u = _pack_u(w[:, mla_in:mla_in + rwkv_in]).astype(bf16)
    w_g = w[:, mla_in + rwkv_in:].astype(bf16)
    mup = _pack_u(mu_prev)
    mun = _pack_u(mu_next)

    wq3 = mla_w_uq[0].reshape(Q_LORA, MLA_HEADS, QK_DIM)
    wq_rope = wq3[:, :, NOPE:]
    w_q = jnp.concatenate([wq3[:, :, :NOPE].reshape(Q_LORA, -1),
                           wq_rope.reshape(Q_LORA, -1),
                           wq_rope[:, :, swap].reshape(Q_LORA, -1)], axis=1).astype(bf16)
    wkv3 = mla_w_ukv[0].reshape(KV_LORA, MLA_HEADS, NOPE + V_DIM)
    w_kv = jnp.concatenate([wkv3[:, :, :NOPE].reshape(KV_LORA, -1),
                            wkv3[:, :, NOPE:].reshape(KV_LORA, -1)], axis=1).astype(bf16)
    cos, sin = _rope_tables()

    qn, qr, kn, v, kr = _mla_front(x, norm_pre_mix, w_mla, mla_q_norm, w_q, mla_kv_norm, w_kv, cos, sin)
    ya_p = _attention(qn, qr, kn, kr, v, 0, T_PROMPT, 1)
    ya_s = _attention(qn, qr, kn, kr, v, T_PROMPT, S_SAMPLE, B_SAMPLE)
    ya = jnp.concatenate([ya_p, ya_s], axis=0)

    u = _norm_matmul(x, norm_pre_mix, w_u, 768, f32, False, "proj_rwkv")
    head_blk = np.arange(GROUP) // RWKV_HEAD
    bd = jnp.asarray(head_blk[:, None] == head_blk[None, :], dtype=bf16)
    vecs = (rwkv_w0_f, rwkv_w0_b, rwkv_a0_f, rwkv_a0_b, rwkv_k_k, rwkv_k_a,
            rwkv_r_k.reshape(1, D_MODEL))
    loras = tuple(_pad_rows(a[0]).astype(bf16) for a in (rwkv_w2_f, rwkv_w2_b, rwkv_a2_f, rwkv_a2_b))
    (r, vv, kk, k_f, k_b, b_f, b_b, lw_f, lw_b, g, bonus) = _rwkv_prep(
        u, mup, mun, vecs, loras, rwkv_g2[0].astype(bf16), bd)
    scan = lambda kx, bx, lwx, rev: jnp.concatenate(
        [_wkv_scan(r, kx, vv, kk, bx, lwx, 0, T_PROMPT, 1, rev),
         _wkv_scan(r, kx, vv, kk, bx, lwx, T_PROMPT, S_SAMPLE, B_SAMPLE, rev)], axis=0)
    y_f = scan(k_f, b_f, lw_f, False)
    y_b = scan(k_b, b_b, lw_b, True)

    gates = _norm_matmul(x, norm_pre_mix, w_g, 1024, bf16, True, "proj_gates")
    wb = w_branch[0].astype(bf16)
    o = _branch(ya, y_f, y_b, bonus, g, gates, rwkv_ln_w, rwkv_ln_b, bd,
                wb[:D_MODEL], wb[D_MODEL:])
    x1 = _out_proj(o, x, w_out[0].astype(bf16), norm_post_mix)
    out = _mlp(x1, norm_pre_mlp, w_mlp_up[0].astype(bf16), w_mlp_down[0].astype(bf16), norm_post_mlp)
    return (out[:T_PROMPT].reshape(1, T_PROMPT, D_MODEL),
            out[T_PROMPT:].reshape(B_SAMPLE, S_SAMPLE, D_MODEL))
```

```python
import functools

import numpy as np
import jax
import jax.numpy as jnp
from jax import lax
from jax.experimental import pallas as pl
from jax.experimental.pallas import tpu as pltpu

f32 = jnp.float32
bf16 = jnp.bfloat16

D_MODEL = 2048
T_PROMPT = 8192
S_SAMPLE = 2048
B_SAMPLE = 4
T_ALL = T_PROMPT + B_SAMPLE * S_SAMPLE

MLA_HEADS = 16
Q_LORA = 768
KV_LORA = 512
NOPE = 128
ROPE = 64
V_DIM = 128
QK_DIM = NOPE + ROPE
ROPE_THETA = 10000.0
LOG2_E = 1.4426950408889634

RWKV_HEAD = 64
RWKV_HEADS = 32
LORA_PAD = 128
GATE_LORA = 256
U_PACKED = 3 * D_MODEL + 4 * LORA_PAD + GATE_LORA

D_FF = 4 * D_MODEL
NORM_EPS = 1e-6
GN_EPS = 64e-5

CHUNK = 64
GROUP = 256
N_GROUPS = D_MODEL // GROUP
VMEM_LIMIT = 56 * 1024 * 1024


def _rms(x, g):
    return x * lax.rsqrt(jnp.mean(x * x, axis=-1, keepdims=True) + NORM_EPS) * g


def _dot(a, b):
    return jnp.dot(a, b, preferred_element_type=f32)


def _dot_nt(a, b):
    return lax.dot_general(a, b, (((1,), (1,)), ((), ())), preferred_element_type=f32)


def _dot_tn(a, b):
    return lax.dot_general(a, b, (((0,), (0,)), ((), ())), preferred_element_type=f32)


def _split_dot(x, w):
    hi = x.astype(bf16)
    lo = (x - hi.astype(f32)).astype(bf16)
    return _dot(hi, w) + _dot(lo, w)


def _const_spec(shape):
    return pl.BlockSpec(shape, lambda *_: (0,) * len(shape), pipeline_mode=pl.Buffered(1))


def _mla_front_kernel(x_ref, g_ref, wm_ref, qg_ref, wq_ref, kvg_ref, wkv_ref, cos_ref, sin_ref,
                      qn_ref, qr_ref, kn_ref, v_ref, kr_ref):
    h = _rms(x_ref[...], g_ref[...]).astype(bf16)
    pm = _dot(h, wm_ref[...])
    cos = cos_ref[...]
    sin = sin_ref[...]
    kr_ref[...] = (pm[:, 1280:1408] * cos + pm[:, 1408:1536] * sin).astype(bf16)
    qin = _rms(pm[:, :Q_LORA], qg_ref[...]).astype(bf16)
    kvin = _rms(pm[:, Q_LORA:Q_LORA + KV_LORA], kvg_ref[...]).astype(bf16)
    scale = QK_DIM ** -0.5 * LOG2_E
    cw = 512
    for c in range(D_MODEL // cw):
        q = _dot(qin, wq_ref[:, c * cw:(c + 1) * cw])
        qn_ref[:, c * cw:(c + 1) * cw] = (q * scale).astype(bf16)
    cos4 = jnp.concatenate([cos] * (cw // 128), axis=1)
    sin4 = jnp.concatenate([sin] * (cw // 128), axis=1)
    nrope = MLA_HEADS * ROPE
    for c in range(nrope // cw):
        qa = _dot(qin, wq_ref[:, D_MODEL + c * cw:D_MODEL + (c + 1) * cw])
        qb = _dot(qin, wq_ref[:, D_MODEL + nrope + c * cw:D_MODEL + nrope + (c + 1) * cw])
        qr_ref[:, c * cw:(c + 1) * cw] = ((qa * cos4 + qb * sin4) * scale).astype(bf16)
    for c in range(D_MODEL // cw):
        kn_ref[:, c * cw:(c + 1) * cw] = _dot(kvin, wkv_ref[:, c * cw:(c + 1) * cw]).astype(bf16)
        v_ref[:, c * cw:(c + 1) * cw] = _dot(
            kvin, wkv_ref[:, D_MODEL + c * cw:D_MODEL + (c + 1) * cw]).astype(bf16)


def _mla_front(x, g, w_mla, qg, w_q, kvg, w_kv, cos, sin, tm=512):
    nq = w_q.shape[1]
    row = lambda w: pl.BlockSpec((tm, w), lambda i: (i, 0))
    return pl.pallas_call(
        _mla_front_kernel,
        grid=(T_ALL // tm,),
        in_specs=[row(D_MODEL), _const_spec((1, D_MODEL)), _const_spec(w_mla.shape),
                  _const_spec((1, Q_LORA)), _const_spec((Q_LORA, nq)),
                  _const_spec((1, KV_LORA)), _const_spec(w_kv.shape), row(128), row(128)],
        out_specs=[row(D_MODEL), row(MLA_HEADS * ROPE), row(D_MODEL), row(D_MODEL), row(128)],
        out_shape=[jax.ShapeDtypeStruct((T_ALL, D_MODEL), bf16),
                   jax.ShapeDtypeStruct((T_ALL, MLA_HEADS * ROPE), bf16),
                   jax.ShapeDtypeStruct((T_ALL, D_MODEL), bf16),
                   jax.ShapeDtypeStruct((T_ALL, D_MODEL), bf16),
                   jax.ShapeDtypeStruct((T_ALL, 128), bf16)],
        compiler_params=pltpu.CompilerParams(dimension_semantics=("parallel",),
                                             vmem_limit_bytes=VMEM_LIMIT),
        name="mla_front",
    )(x, g, w_mla, qg, w_q, kvg, w_kv, cos, sin)


def _norm_matmul_kernel(x_ref, g_ref, w_ref, o_ref, h_ref, *, sigmoid):
    @pl.when(pl.program_id(1) == 0)
    def _():
        h_ref[...] = _rms(x_ref[...], g_ref[...]).astype(bf16)

    acc = _dot(h_ref[...], w_ref[...])
    if sigmoid:
        acc = jax.nn.sigmoid(acc)
    o_ref[...] = acc.astype(o_ref.dtype)


def _norm_matmul(x, g, w, tn, out_dtype, sigmoid, name, tm=1024):
    n = w.shape[1]
    return pl.pallas_call(
        functools.partial(_norm_matmul_kernel, sigmoid=sigmoid),
        grid=(T_ALL // tm, n // tn),
        in_specs=[pl.BlockSpec((tm, D_MODEL), lambda i, j: (i, 0)),
                  _const_spec((1, D_MODEL)),
                  pl.BlockSpec((D_MODEL, tn), lambda i, j: (0, j))],
        out_specs=pl.BlockSpec((tm, tn), lambda i, j: (i, j)),
        out_shape=jax.ShapeDtypeStruct((T_ALL, n), out_dtype),
        scratch_shapes=[pltpu.VMEM((tm, D_MODEL), bf16)],
        compiler_params=pltpu.CompilerParams(dimension_semantics=("parallel", "arbitrary"),
                                             vmem_limit_bytes=VMEM_LIMIT),
        name=name,
    )(x, g, w)


def _attn_kernel(qn_ref, qr_ref, kn_ref, kr_ref, v_ref, o_ref, kcat_ref, vt_ref, s_ref, *,
                 seq, tq, tk, unroll):
    head = pl.program_id(1)
    nsteps = seq // tk

    @pl.when(pl.program_id(2) == 0)
    def _():
        kcat_ref[:, :NOPE] = kn_ref[...]
        kcat_ref[:, NOPE:] = kr_ref[...]
        for c in range(seq // 512):
            cs = slice(c * 512, (c + 1) * 512)
            vt_ref[:, cs] = jnp.transpose(v_ref[cs, :].astype(f32)).astype(bf16)

    lane = lax.broadcasted_iota(jnp.int32, (tq, 128), 1)
    keep = (lane // ROPE) == (head % 2)
    qr = jnp.where(keep, qr_ref[...].astype(f32), 0.0).astype(bf16)
    q = jnp.concatenate([qn_ref[...], qr], axis=1)

    def scores(j, slot):
        rows = pl.ds(pl.multiple_of(j * tk, tk), tk)
        s_ref[slot] = _dot_nt(kcat_ref[rows, :], q)

    def absorb(j, slot, carry):
        m, l8, acc = carry
        s = s_ref[slot]
        m_loc = jnp.max(s.reshape(tk // 8, 8, tq), axis=0)
        m_new = jnp.maximum(m, jnp.max(m_loc, axis=0, keepdims=True))
        alpha = jnp.exp2(m - m_new)
        p = jnp.exp2(s - m_new)
        l8 = alpha * l8 + jnp.sum(p.reshape(tk // 8, 8, tq), axis=0)
        cols = pl.ds(pl.multiple_of(j * tk, tk), tk)
        acc = alpha * acc + _dot(vt_ref[:, cols], p.astype(bf16))
        return m_new, l8, acc

    def run(j0, carry, is_tail):
        for u in range(unroll):
            if not (is_tail and u == unroll - 1):
                scores(j0 + u + 1, (u + 1) % 2)
            carry = absorb(j0 + u, u % 2, carry)
        return carry

    carry = (jnp.full((1, tq), -jnp.inf, f32), jnp.zeros((8, tq), f32), jnp.zeros((V_DIM, tq), f32))
    scores(0, 0)
    carry = lax.fori_loop(0, nsteps // unroll - 1, lambda i, c: run(i * unroll, c, False), carry)
    _, l8, acc = run(nsteps - unroll, carry, True)
    out_t = acc / jnp.sum(l8, axis=0, keepdims=True)
    o_ref[...] = jnp.transpose(out_t).astype(o_ref.dtype)


def _attention(qn, qr, kn, kr, v, row_off, seq, nb, tq=256, tk=512, unroll=4):
    qoff = row_off // tq
    soff = row_off // seq
    nq = seq // tq
    return pl.pallas_call(
        functools.partial(_attn_kernel, seq=seq, tq=tq, tk=tk, unroll=unroll),
        grid=(nb, MLA_HEADS, nq),
        in_specs=[pl.BlockSpec((tq, NOPE), lambda b, h, i: (qoff + b * nq + i, h)),
                  pl.BlockSpec((tq, 128), lambda b, h, i: (qoff + b * nq + i, h // 2)),
                  pl.BlockSpec((seq, NOPE), lambda b, h, i: (soff + b, h)),
                  pl.BlockSpec((seq, 128), lambda b, h, i: (soff + b, 0)),
                  pl.BlockSpec((seq, V_DIM), lambda b, h, i: (soff + b, h))],
        out_specs=pl.BlockSpec((tq, V_DIM), lambda b, h, i: (b * nq + i, h)),
        out_shape=jax.ShapeDtypeStruct((nb * seq, D_MODEL), bf16),
        scratch_shapes=[pltpu.VMEM((seq, 2 * NOPE), bf16), pltpu.VMEM((V_DIM, seq), bf16),
                        pltpu.VMEM((2, tk, tq), f32)],
        compiler_params=pltpu.CompilerParams(
            dimension_semantics=("parallel", "parallel", "arbitrary"),
            vmem_limit_bytes=VMEM_LIMIT),
        name=f"mla_attention_s{seq}",
    )(qn, qr, kn, kr, v)


def _seg_sum(x, bd):
    cols = []
    for c in range(x.shape[1] // GROUP):
        cols.append(_split_dot(x[:, c * GROUP:(c + 1) * GROUP], bd))
    return jnp.concatenate(cols, axis=1)


def _store_groups(ref, x):
    for g in range(N_GROUPS):
        ref[g] = x[:, g * GROUP:(g + 1) * GROUP]


def _rwkv_prep_kernel(u_ref, up_ref, un_ref, mup_ref, mun_ref,
                      w0f_ref, w2f_ref, w0b_ref, w2b_ref, a0f_ref, a2f_ref, a0b_ref, a2b_ref,
                      g2_ref, kkw_ref, kaw_ref, rk_ref, bd_ref,
                      r_ref, v_ref, kk_ref, kf_ref, kb_ref, bf_ref, bb_ref, lwf_ref, lwb_ref,
                      g_ref, bonus_ref, *, tm):
    t0 = pl.program_id(0) * tm
    t1 = t0 + tm
    first = jnp.logical_or(t0 == 0, jnp.logical_and(t0 >= T_PROMPT, (t0 - T_PROMPT) % S_SAMPLE == 0))
    last = jnp.logical_or(t1 == T_PROMPT, jnp.logical_and(t1 > T_PROMPT, (t1 - T_PROMPT) % S_SAMPLE == 0))
    keep_prev = jnp.where(first, 0.0, 1.0)
    keep_next = jnp.where(last, 0.0, 1.0)
    row = lax.broadcasted_iota(jnp.int32, (tm, 1), 0)

    def shifted(lo, hi):
        u = u_ref[:, lo:hi]
        prev_row = up_ref[7:8, lo:hi] * keep_prev
        next_row = un_ref[0:1, lo:hi] * keep_next
        prev = jnp.where(row == 0, prev_row, pltpu.roll(u, 1, axis=0))
        nxt = jnp.where(row == tm - 1, next_row, pltpu.roll(u, tm - 1, axis=0))
        return u + mup_ref[:, lo:hi] * (prev - u) + mun_ref[:, lo:hi] * (nxt - u)

    d = D_MODEL
    r = shifted(0, d)
    k = shifted(d, 2 * d)
    v = shifted(2 * d, 3 * d)
    small = shifted(3 * d, U_PACKED)
    xw_f = small[:, 0:128]
    xw_b = small[:, 128:256]
    xa_f = small[:, 256:384]
    xa_b = small[:, 384:512]
    xg = small[:, 512:768]

    def neg_exp_logw(xw, w0_ref, w2_ref):
        z = w0_ref[...] + _dot(jnp.tanh(xw).astype(bf16), w2_ref[...])
        logw = -jax.nn.softplus(-z) - 0.5
        return -jnp.exp(logw)

    _store_groups(lwf_ref, neg_exp_logw(xw_f, w0f_ref, w2f_ref))
    _store_groups(lwb_ref, neg_exp_logw(xw_b, w0b_ref, w2b_ref))
    a_f = jax.nn.sigmoid(a0f_ref[...] + _dot(xa_f.astype(bf16), a2f_ref[...]))
    a_b = jax.nn.sigmoid(a0b_ref[...] + _dot(xa_b.astype(bf16), a2b_ref[...]))
    g_ref[...] = _dot(jax.nn.sigmoid(xg).astype(bf16), g2_ref[...]).astype(bf16)

    bd = bd_ref[...]
    kk = k * kkw_ref[...]
    norm = jnp.sqrt(_seg_sum(kk * kk, bd))
    kk = kk / jnp.maximum(norm, 1e-12)
    ka = kaw_ref[...]
    k_f = k * (1.0 + (a_f - 1.0) * ka)
    k_b = k * (1.0 + (a_b - 1.0) * ka)
    bonus = _seg_sum(r * (k_f + k_b) * rk_ref[...], bd) * v

    _store_groups(r_ref, r.astype(bf16))
    _store_groups(v_ref, v.astype(bf16))
    _store_groups(kk_ref, kk.astype(bf16))
    _store_groups(kf_ref, k_f.astype(bf16))
    _store_groups(kb_ref, k_b.astype(bf16))
    _store_groups(bf_ref, (kk * a_f).astype(bf16))
    _store_groups(bb_ref, (kk * a_b).astype(bf16))
    bonus_ref[...] = bonus.astype(bf16)


def _rwkv_prep(u, mup, mun, vecs, loras, g2, bd, tm=128):
    hb = tm // 8
    nblk8 = T_ALL // 8
    vec = _const_spec((1, D_MODEL))
    lora = _const_spec((LORA_PAD, D_MODEL))
    w0f, w0b, a0f, a0b, kkw, kaw, rk = vecs
    w2f, w2b, a2f, a2b = loras
    out_bf = jax.ShapeDtypeStruct((T_ALL, D_MODEL), bf16)
    grp_bf = jax.ShapeDtypeStruct((N_GROUPS, T_ALL, GROUP), bf16)
    grp_f = jax.ShapeDtypeStruct((N_GROUPS, T_ALL, GROUP), f32)
    row = pl.BlockSpec((tm, D_MODEL), lambda i: (i, 0))
    grp = pl.BlockSpec((N_GROUPS, tm, GROUP), lambda i: (0, i, 0))
    return pl.pallas_call(
        functools.partial(_rwkv_prep_kernel, tm=tm),
        grid=(T_ALL // tm,),
        in_specs=[pl.BlockSpec((tm, U_PACKED), lambda i: (i, 0)),
                  pl.BlockSpec((8, U_PACKED), lambda i: (jnp.maximum(i * hb - 1, 0), 0)),
                  pl.BlockSpec((8, U_PACKED), lambda i: (jnp.minimum((i + 1) * hb, nblk8 - 1), 0)),
                  _const_spec((1, U_PACKED)), _const_spec((1, U_PACKED)),
                  vec, lora, vec, lora, vec, lora, vec, lora,
                  _const_spec((GATE_LORA, D_MODEL)), vec, vec, vec, _const_spec((GROUP, GROUP))],
        out_specs=[grp] * 9 + [row, row],
        out_shape=[grp_bf] * 7 + [grp_f, grp_f, out_bf, out_bf],
        compiler_params=pltpu.CompilerParams(dimension_semantics=("parallel",),
                                             vmem_limit_bytes=VMEM_LIMIT),
        name="rwkv_prep",
    )(u, u, u, mup, mun, w0f, w2f, w0b, w2b, a0f, a2f, a0b, a2b, g2, kkw, kaw, rk, bd)


def _wkv_masks(reverse):
    C = CHUNK
    t_i = lax.broadcasted_iota(jnp.int32, (C, GROUP), 0)
    s_i = lax.broadcasted_iota(jnp.int32, (C, GROUP), 1) % C
    strict = (s_i > t_i) if reverse else (s_i < t_i)
    incl = (s_i >= t_i) if reverse else (s_i <= t_i)
    eye = jnp.where(s_i == t_i, 1.0, 0.0)
    tt = lax.broadcasted_iota(jnp.int32, (C, C), 0)
    ss = lax.broadcasted_iota(jnp.int32, (C, C), 1)
    tri = jnp.where((ss >= tt) if reverse else (ss <= tt), 1.0, 0.0).astype(bf16)
    return strict, incl, eye, tri


def _wkv_chunk_operators(chains, bdmask):
    C = CHUNK
    n = range(len(chains))
    rs, ks, vs, kks, bs, lws, masks, revs = zip(*chains)
    strict = [m[0] for m in masks]
    incl = [m[1] for m in masks]
    eye = [m[2] for m in masks]
    tri = [m[3] for m in masks]
    last = [0 if rev else C - 1 for rev in revs]

    def bd(x):
        return jnp.where(bdmask, jnp.concatenate([x] * 4, axis=0), 0.0).astype(bf16)

    def ss_mul(x, y_bd):
        return _dot(x.astype(bf16), y_bd)

    lw_hi = [lws[i].astype(bf16) for i in n]
    lw_lo = [(lws[i] - lw_hi[i].astype(f32)).astype(bf16) for i in n]
    cum = [_dot(tri[i], lw_hi[i]) + _dot(tri[i], lw_lo[i]) for i in n]
    w_inv = [jnp.exp(-cum[i]) for i in n]
    w_rem = [jnp.exp(cum[i][last[i]:last[i] + 1, :] - cum[i]) for i in n]
    a_t = [-kks[i] * jnp.exp(cum[i] - lws[i]) for i in n]
    r_t = [rs[i] * jnp.exp(cum[i]) for i in n]

    lhs = [jnp.concatenate([a_t[i], r_t[i]], axis=0).astype(bf16) for i in n]
    ab = [_dot_nt(lhs[i], bd(bs[i] * w_inv[i])) for i in n]
    ak = [_dot_nt(lhs[i], bd(ks[i] * w_inv[i])) for i in n]
    a_ab = [jnp.where(strict[i], ab[i][:C], 0.0) for i in n]
    a_rb = [jnp.where(incl[i], ab[i][C:], 0.0) for i in n]
    a_k = [jnp.concatenate([jnp.where(strict[i], ak[i][:C], 0.0),
                            jnp.where(incl[i], ak[i][C:], 0.0)], axis=0) for i in n]

    tinv = [eye[i] + a_ab[i] for i in n]
    pw = [ss_mul(a_ab[i], bd(a_ab[i])) for i in n]
    for _ in range(4):
        both = [ss_mul(jnp.concatenate([tinv[i], pw[i]], axis=0), bd(pw[i])) for i in n]
        tinv = [tinv[i] + both[i][:C] for i in n]
        pw = [both[i][C:] for i in n]
    tinv = [tinv[i] + ss_mul(tinv[i], bd(pw[i])) for i in n]

    zr = [ss_mul(a_k[i], bd(vs[i])) for i in n]
    pq = [ss_mul(tinv[i], jnp.concatenate([bd(a_t[i]), bd(zr[i][:C])], axis=1)) for i in n]
    p = [pq[i][:, :GROUP] for i in n]
    q = [pq[i][:, GROUP:] for i in n]
    yy = [ss_mul(a_rb[i], jnp.concatenate([bd(p[i]), bd(q[i])], axis=1)) for i in n]
    y_op = [r_t[i] + yy[i][:, :GROUP] for i in n]
    y_c = [yy[i][:, GROUP:] + zr[i][C:] for i in n]
    py = [jnp.concatenate([p[i], y_op[i]], axis=0).astype(bf16) for i in n]
    bkt = [jnp.transpose(jnp.concatenate([bs[i] * w_rem[i], ks[i] * w_rem[i]], axis=0)).astype(bf16)
           for i in n]
    w_tot = [jnp.broadcast_to(jnp.exp(jnp.transpose(cum[i])[:, last[i]:last[i] + 1]), (GROUP, 128))
             for i in n]
    return [(py[i], q[i], y_c[i], bkt[i], w_tot[i]) for i in n]


def _wkv_kernel(rf_ref, vf_ref, kkf_ref, kf_ref, bf_ref, lwf_ref,
                rb_ref, vb_ref, kkb_ref, kb_ref, bb_ref, lwb_ref,
                yf_ref, yb_ref,
                h_ref, py_ref, q_ref, yc_ref, bkt_ref, wt_ref, *, cb, ng):
    C = CHUNK

    @pl.when(pl.program_id(2) == 0)
    def _():
        h_ref[...] = jnp.zeros_like(h_ref)

    br = lax.broadcasted_iota(jnp.int32, (GROUP, GROUP), 0) // RWKV_HEAD
    bc = lax.broadcasted_iota(jnp.int32, (GROUP, GROUP), 1) // RWKV_HEAD
    bdmask = br == bc
    dirs = ((rf_ref, vf_ref, kkf_ref, kf_ref, bf_ref, lwf_ref, yf_ref, _wkv_masks(False)),
            (rb_ref, vb_ref, kkb_ref, kb_ref, bb_ref, lwb_ref, yb_ref, _wkv_masks(True)))

    def operators(g, carry):
        chains, slots = [], []
        for j in range(cb):
            rows = pl.ds(j * C, C)
            for d, (r_ref, v_ref, kk_ref, k_ref, b_ref, lw_ref, _, masks) in enumerate(dirs):
                chains.append((r_ref[g, rows, :].astype(f32), k_ref[g, rows, :].astype(f32),
                               v_ref[g, rows, :].astype(f32), kk_ref[g, rows, :].astype(f32),
                               b_ref[g, rows, :].astype(f32), lw_ref[g, rows, :], masks, d == 1))
                slots.append((d, j))
        for (d, j), (py, q, y_c, bkt, wt) in zip(slots, _wkv_chunk_operators(chains, bdmask)):
            py_ref[g, d, j] = py
            q_ref[g, d, j] = q
            yc_ref[g, d, j] = y_c
            bkt_ref[g, d, j] = bkt
            wt_ref[g, d, j] = wt
        return carry

    lax.fori_loop(0, ng, operators, 0)

    def recur(j, carry):
        jd = (j, cb - 1 - j)
        rows = [pl.ds(pl.multiple_of(jd[d] * C, C), C) for d in range(2)]
        gd = [(g, d) for g in range(ng) for d in range(2)]
        h = [h_ref[g, d] for g, d in gd]
        uy = [_dot(py_ref[g, d, jd[d]], h[i].astype(bf16)) for i, (g, d) in enumerate(gd)]
        uv = []
        for i, (g, d) in enumerate(gd):
            dirs[d][6][g, rows[d], :] = uy[i][C:] + yc_ref[g, d, jd[d]]
            u = uy[i][:C] + q_ref[g, d, jd[d]]
            uv.append(jnp.concatenate([u.astype(bf16), dirs[d][1][g, rows[d], :]], axis=0))
        upd = [_dot(bkt_ref[g, d, jd[d]], uv[i]) for i, (g, d) in enumerate(gd)]
        for i, (g, d) in enumerate(gd):
            wt = wt_ref[g, d, jd[d]]
            h_ref[g, d] = h[i] * jnp.concatenate([wt, wt], axis=1) + jnp.where(bdmask, upd[i], 0.0)
        return carry

    lax.fori_loop(0, cb, recur, 0)


def _wkv_scan(r, v, kk, k_f, b_f, lw_f, k_b, b_b, lw_b, row_off, seq, nb, cb=4, ng=4):
    C = CHUNK
    rb = cb * C
    nblk = seq // rb
    boff = row_off // rb
    fwd = pl.BlockSpec((ng, rb, GROUP), lambda s, g, c: (g, boff + s * nblk + c, 0))
    bwd = pl.BlockSpec((ng, rb, GROUP), lambda s, g, c: (g, boff + s * nblk + nblk - 1 - c, 0))
    out_f = pl.BlockSpec((ng, rb, GROUP), lambda s, g, c: (g, s * nblk + c, 0))
    out_b = pl.BlockSpec((ng, rb, GROUP), lambda s, g, c: (g, s * nblk + nblk - 1 - c, 0))
    out = jax.ShapeDtypeStruct((N_GROUPS, nb * seq, GROUP), f32)
    return pl.pallas_call(
        functools.partial(_wkv_kernel, cb=cb, ng=ng),
        grid=(nb, N_GROUPS // ng, nblk),
        in_specs=[fwd] * 6 + [bwd] * 6,
        out_specs=[out_f, out_b],
        out_shape=[out, out],
        scratch_shapes=[pltpu.VMEM((ng, 2, GROUP, GROUP), f32),
                        pltpu.VMEM((ng, 2, cb, 2 * C, GROUP), bf16),
                        pltpu.VMEM((ng, 2, cb, C, GROUP), f32),
                        pltpu.VMEM((ng, 2, cb, C, GROUP), f32),
                        pltpu.VMEM((ng, 2, cb, GROUP, 2 * C), bf16),
                        pltpu.VMEM((ng, 2, cb, GROUP, 128), f32)],
        compiler_params=pltpu.CompilerParams(
            dimension_semantics=("parallel", "parallel", "arbitrary"),
            vmem_limit_bytes=VMEM_LIMIT),
        name=f"wkv_s{seq}",
    )(r, v, kk, k_f, b_f, lw_f, r, v, kk, k_b, b_b, lw_b)


def _branch_kernel(ya_ref, yf_ref, yb_ref, bonus_ref, g_ref, g1_ref, g2_ref, lnw_ref, lnb_ref,
                   bd_ref, wa_ref, wb_ref, o_ref):
    bd = bd_ref[...]
    y = jnp.concatenate([yf_ref[g] + yb_ref[g] for g in range(N_GROUPS)], axis=1)
    inv_n = 1.0 / RWKV_HEAD
    mean = _seg_sum(y, bd) * inv_n
    dlt = y - mean
    var = _seg_sum(dlt * dlt, bd) * inv_n
    yn = dlt * lax.rsqrt(var + GN_EPS) * lnw_ref[...] + lnb_ref[...]
    yb = ((yn + bonus_ref[...].astype(f32)) * g_ref[...].astype(f32)).astype(bf16)
    ya = ya_ref[...]
    cw = 512
    for c in range(D_MODEL // cw):
        cs = slice(c * cw, (c + 1) * cw)
        oa = _dot(ya, wa_ref[:, cs])
        ob = _dot(yb, wb_ref[:, cs])
        o_ref[:, cs] = (g1_ref[:, cs].astype(f32) * oa + g2_ref[:, cs].astype(f32) * ob).astype(bf16)


def _branch(ya, yf, yb, bonus, g, gates, lnw, lnb, bd, wa, wb, tm=256):
    row = pl.BlockSpec((tm, D_MODEL), lambda i: (i, 0))
    grp = pl.BlockSpec((N_GROUPS, tm, GROUP), lambda i: (0, i, 0))
    return pl.pallas_call(
        _branch_kernel,
        grid=(T_ALL // tm,),
        in_specs=[row, grp, grp, row, row,
                  pl.BlockSpec((tm, D_MODEL), lambda i: (i, 0)),
                  pl.BlockSpec((tm, D_MODEL), lambda i: (i, 1)),
                  _const_spec((1, D_MODEL)), _const_spec((1, D_MODEL)), _const_spec((GROUP, GROUP)),
                  _const_spec((D_MODEL, D_MODEL)), _const_spec((D_MODEL, D_MODEL))],
        out_specs=row,
        out_shape=jax.ShapeDtypeStruct((T_ALL, D_MODEL), bf16),
        compiler_params=pltpu.CompilerParams(dimension_semantics=("parallel",),
                                             vmem_limit_bytes=VMEM_LIMIT),
        name="branch_mix",
    )(ya, yf, yb, bonus, g, gates, gates, lnw, lnb, bd, wa, wb)


def _out_proj_kernel(o_ref, x_ref, w_ref, g_ref, x1_ref):
    mix = _dot(o_ref[...], w_ref[...])
    x1_ref[...] = x_ref[...] + _rms(mix, g_ref[...])


def _out_proj(o, x, w, g, tm=512):
    row = pl.BlockSpec((tm, D_MODEL), lambda i: (i, 0))
    return pl.pallas_call(
        _out_proj_kernel,
        grid=(T_ALL // tm,),
        in_specs=[row, row, _const_spec((D_MODEL, D_MODEL)), _const_spec((1, D_MODEL))],
        out_specs=row,
        out_shape=jax.ShapeDtypeStruct((T_ALL, D_MODEL), f32),
        compiler_params=pltpu.CompilerParams(dimension_semantics=("parallel",),
                                             vmem_limit_bytes=VMEM_LIMIT),
        name="out_proj",
    )(o, x, w, g)


def _mlp_kernel(x_ref, gpre_ref, wu_ref, wd_ref, gpost_ref, o_ref, xn_ref, acc_ref):
    j = pl.program_id(1)

    @pl.when(j == 0)
    def _():
        xn_ref[...] = _rms(x_ref[...], gpre_ref[...]).astype(bf16)
        acc_ref[...] = jnp.zeros_like(acc_ref)

    h = jnp.maximum(_dot(xn_ref[...], wu_ref[...]), 0.0)
    acc_ref[...] += _dot((h * h).astype(bf16), wd_ref[...])

    @pl.when(j == pl.num_programs(1) - 1)
    def _():
        o_ref[...] = x_ref[...] + _rms(acc_ref[...], gpost_ref[...])


def _mlp(x1, gpre, w_up, w_down, gpost, tm=512, tf=1024):
    row = pl.BlockSpec((tm, D_MODEL), lambda i, j: (i, 0))
    return pl.pallas_call(
        _mlp_kernel,
        grid=(T_ALL // tm, D_FF // tf),
        in_specs=[row, _const_spec((1, D_MODEL)),
                  pl.BlockSpec((D_MODEL, tf), lambda i, j: (0, j)),
                  pl.BlockSpec((tf, D_MODEL), lambda i, j: (j, 0)),
                  _const_spec((1, D_MODEL))],
        out_specs=row,
        out_shape=jax.ShapeDtypeStruct((T_ALL, D_MODEL), f32),
        scratch_shapes=[pltpu.VMEM((tm, D_MODEL), bf16), pltpu.VMEM((tm, D_MODEL), f32)],
        compiler_params=pltpu.CompilerParams(dimension_semantics=("parallel", "arbitrary"),
                                             vmem_limit_bytes=VMEM_LIMIT),
        name="mlp",
    )(x1, gpre, w_up, w_down, gpost)


def _pad_lanes(a, width):
    return jnp.pad(a, [(0, 0)] * (a.ndim - 1) + [(0, width - a.shape[-1])])


def _pack_u(a):
    d = D_MODEL
    cuts = [3 * d, 3 * d + 96, 3 * d + 192, 3 * d + 288, 3 * d + 384]
    return jnp.concatenate(
        [a[..., :cuts[0]]]
        + [_pad_lanes(a[..., cuts[i]:cuts[i + 1]], LORA_PAD) for i in range(4)]
        + [a[..., cuts[4]:]], axis=-1)


def _pad_rows(w):
    return jnp.pad(w, [(0, LORA_PAD - w.shape[0]), (0, 0)])


def _rope_tables():
    inv = 1.0 / (ROPE_THETA ** (jnp.arange(0, ROPE, 2, dtype=f32) / ROPE))
    pos = jnp.concatenate([jnp.arange(T_PROMPT, dtype=f32),
                           jnp.tile(jnp.arange(S_SAMPLE, dtype=f32), B_SAMPLE)])
    ang = pos[:, None] * inv[None, :]
    cos, sin = jnp.cos(ang), jnp.sin(ang)
    cos = jnp.concatenate([cos, cos] * 2, axis=1)
    sin = jnp.concatenate([-sin, sin] * 2, axis=1)
    return cos, sin


def kernel(x_prompt, x_sample, norm_pre_mix, w_in, mu_prev, mu_next, mla_q_norm, mla_w_uq, mla_kv_norm, mla_w_ukv, rwkv_w0_f, rwkv_w2_f, rwkv_w0_b, rwkv_w2_b, rwkv_a0_f, rwkv_a2_f, rwkv_a0_b, rwkv_a2_b, rwkv_g2, rwkv_k_k, rwkv_k_a, rwkv_r_k, rwkv_ln_w, rwkv_ln_b, w_branch, w_out, norm_post_mix, norm_pre_mlp, w_mlp_up, w_mlp_down, norm_post_mlp):
    x = jnp.concatenate([x_prompt.reshape(T_PROMPT, D_MODEL),
                         x_sample.reshape(B_SAMPLE * S_SAMPLE, D_MODEL)], axis=0)

    swap = np.concatenate([np.arange(ROPE // 2, ROPE), np.arange(ROPE // 2)])
    w = w_in[0]
    mla_in = Q_LORA + KV_LORA + ROPE
    rwkv_in = 3 * D_MODEL + 4 * 96 + GATE_LORA
    w_kr = w[:, Q_LORA + KV_LORA:mla_in]
    w_krs = w_kr[:, swap]
    w_mla = jnp.concatenate([w[:, :Q_LORA + KV_LORA], w_kr, w_kr, w_krs, w_krs], axis=1).astype(bf16)
    w_u = _pack_u(w[:, mla_in:mla_in + rwkv_in]).astype(bf16)
    w_g = w[:, mla_in + rwkv_in:].astype(bf16)
    mup = _pack_u(mu_prev)
    mun = _pack_u(mu_next)

    wq3 = mla_w_uq[0].reshape(Q_LORA, MLA_HEADS, QK_DIM)
    wq_rope = wq3[:, :, NOPE:]
    w_q = jnp.concatenate([wq3[:, :, :NOPE].reshape(Q_LORA, -1),
                           wq_rope.reshape(Q_LORA, -1),
                           wq_rope[:, :, swap].reshape(Q_LORA, -1)], axis=1).astype(bf16)
    wkv3 = mla_w_ukv[0].reshape(KV_LORA, MLA_HEADS, NOPE + V_DIM)
    w_kv = jnp.concatenate([wkv3[:, :, :NOPE].reshape(KV_LORA, -1),
                            wkv3[:, :, NOPE:].reshape(KV_LORA, -1)], axis=1).astype(bf16)
    cos, sin = _rope_tables()

    qn, qr, kn, v, kr = _mla_front(x, norm_pre_mix, w_mla, mla_q_norm, w_q, mla_kv_norm, w_kv, cos, sin)
    ya_p = _attention(qn, qr, kn, kr, v, 0, T_PROMPT, 1, unroll=8)
    ya_s = _attention(qn, qr, kn, kr, v, T_PROMPT, S_SAMPLE, B_SAMPLE)
    ya = jnp.concatenate([ya_p, ya_s], axis=0)

    u = _norm_matmul(x, norm_pre_mix, w_u, 768, f32, False, "proj_rwkv")
    head_blk = np.arange(GROUP) // RWKV_HEAD
    bd = jnp.asarray(head_blk[:, None] == head_blk[None, :], dtype=bf16)
    vecs = (rwkv_w0_f, rwkv_w0_b, rwkv_a0_f, rwkv_a0_b, rwkv_k_k, rwkv_k_a,
            rwkv_r_k.reshape(1, D_MODEL))
    loras = tuple(_pad_rows(a[0]).astype(bf16) for a in (rwkv_w2_f, rwkv_w2_b, rwkv_a2_f, rwkv_a2_b))
    (r, vv, kk, k_f, k_b, b_f, b_b, lw_f, lw_b, g, bonus) = _rwkv_prep(
        u, mup, mun, vecs, loras, rwkv_g2[0].astype(bf16), bd)
    yf_p, yb_p = _wkv_scan(r, vv, kk, k_f, b_f, lw_f, k_b, b_b, lw_b, 0, T_PROMPT, 1)
    yf_s, yb_s = _wkv_scan(r, vv, kk, k_f, b_f, lw_f, k_b, b_b, lw_b, T_PROMPT, S_SAMPLE, B_SAMPLE)
    y_f = jnp.concatenate([yf_p, yf_s], axis=1)
    y_b = jnp.concatenate([yb_p, yb_s], axis=1)

    gates = _norm_matmul(x, norm_pre_mix, w_g, 1024, bf16, True, "proj_gates")
    wb = w_branch[0].astype(bf16)
    o = _branch(ya, y_f, y_b, bonus, g, gates, rwkv_ln_w, rwkv_ln_b, bd,
                wb[:D_MODEL], wb[D_MODEL:])
    x1 = _out_proj(o, x, w_out[0].astype(bf16), norm_post_mix)
    out = _mlp(x1, norm_pre_mlp, w_mlp_up[0].astype(bf16), w_mlp_down[0].astype(bf16), norm_post_mlp)
    return (out[:T_PROMPT].reshape(1, T_PROMPT, D_MODEL),
            out[T_PROMPT:].reshape(B_SAMPLE, S_SAMPLE, D_MODEL))
```

```python
import functools

import numpy as np
import jax
import jax.numpy as jnp
from jax import lax
from jax.experimental import pallas as pl
from jax.experimental.pallas import tpu as pltpu

f32 = jnp.float32
bf16 = jnp.bfloat16

D_MODEL = 2048
T_PROMPT = 8192
S_SAMPLE = 2048
B_SAMPLE = 4
T_ALL = T_PROMPT + B_SAMPLE * S_SAMPLE

MLA_HEADS = 16
Q_LORA = 768
KV_LORA = 512
NOPE = 128
ROPE = 64
V_DIM = 128
QK_DIM = NOPE + ROPE
ROPE_THETA = 10000.0
LOG2_E = 1.4426950408889634

RWKV_HEAD = 64
RWKV_HEADS = 32
LORA_PAD = 128
GATE_LORA = 256
U_PACKED = 3 * D_MODEL + 4 * LORA_PAD + GATE_LORA

D_FF = 4 * D_MODEL
NORM_EPS = 1e-6
GN_EPS = 64e-5

CHUNK = 64
GROUP = 256
N_GROUPS = D_MODEL // GROUP
VMEM_LIMIT = 56 * 1024 * 1024


def _rms(x, g):
    return x * lax.rsqrt(jnp.mean(x * x, axis=-1, keepdims=True) + NORM_EPS) * g


def _dot(a, b):
    return jnp.dot(a, b, preferred_element_type=f32)


def _dot_nt(a, b):
    return lax.dot_general(a, b, (((1,), (1,)), ((), ())), preferred_element_type=f32)


def _dot_tn(a, b):
    return lax.dot_general(a, b, (((0,), (0,)), ((), ())), preferred_element_type=f32)


def _split_dot(x, w):
    hi = x.astype(bf16)
    lo = (x - hi.astype(f32)).astype(bf16)
    return _dot(hi, w) + _dot(lo, w)


def _const_spec(shape):
    return pl.BlockSpec(shape, lambda *_: (0,) * len(shape), pipeline_mode=pl.Buffered(1))


def _x_specs(tm):
    npb = T_PROMPT // tm
    return [pl.BlockSpec((tm, D_MODEL), lambda i, *_: (jnp.minimum(i, npb - 1), 0)),
            pl.BlockSpec((tm, D_MODEL), lambda i, *_: (jnp.maximum(i - npb, 0), 0))]


def _is_sequence_edge(row):
    return jnp.logical_or(row == 0, jnp.logical_and(row >= T_PROMPT, (row - T_PROMPT) % S_SAMPLE == 0))


def _pick_x(xp_ref, xs_ref):
    tm = xp_ref.shape[0]
    return jnp.where(pl.program_id(0) < T_PROMPT // tm, xp_ref[...], xs_ref[...])


def _mla_front_kernel(xp_ref, xs_ref, g_ref, wm_ref, qg_ref, wq_ref, kvg_ref, wkv_ref, cos_ref, sin_ref,
                      qn_ref, qr_ref, kn_ref, v_ref, kr_ref):
    h = _rms(_pick_x(xp_ref, xs_ref), g_ref[...]).astype(bf16)
    pm = _dot(h, wm_ref[...])
    cos = cos_ref[...]
    sin = sin_ref[...]
    kr_ref[...] = (pm[:, 1280:1408] * cos + pm[:, 1408:1536] * sin).astype(bf16)
    qin = _rms(pm[:, :Q_LORA], qg_ref[...]).astype(bf16)
    kvin = _rms(pm[:, Q_LORA:Q_LORA + KV_LORA], kvg_ref[...]).astype(bf16)
    scale = QK_DIM ** -0.5 * LOG2_E
    cw = 512
    for c in range(D_MODEL // cw):
        q = _dot(qin, wq_ref[:, c * cw:(c + 1) * cw])
        qn_ref[:, c * cw:(c + 1) * cw] = (q * scale).astype(bf16)
    cos4 = jnp.concatenate([cos] * (cw // 128), axis=1)
    sin4 = jnp.concatenate([sin] * (cw // 128), axis=1)
    nrope = MLA_HEADS * ROPE
    for c in range(nrope // cw):
        qa = _dot(qin, wq_ref[:, D_MODEL + c * cw:D_MODEL + (c + 1) * cw])
        qb = _dot(qin, wq_ref[:, D_MODEL + nrope + c * cw:D_MODEL + nrope + (c + 1) * cw])
        qr_ref[:, c * cw:(c + 1) * cw] = ((qa * cos4 + qb * sin4) * scale).astype(bf16)
    for c in range(D_MODEL // cw):
        kn_ref[:, c * cw:(c + 1) * cw] = _dot(kvin, wkv_ref[:, c * cw:(c + 1) * cw]).astype(bf16)
        v_ref[:, c * cw:(c + 1) * cw] = _dot(
            kvin, wkv_ref[:, D_MODEL + c * cw:D_MODEL + (c + 1) * cw]).astype(bf16)


def _mla_front(xp, xs, g, w_mla, qg, w_q, kvg, w_kv, cos, sin, tm=512):
    nq = w_q.shape[1]
    row = lambda w: pl.BlockSpec((tm, w), lambda i: (i, 0))
    return pl.pallas_call(
        _mla_front_kernel,
        grid=(T_ALL // tm,),
        in_specs=_x_specs(tm) + [_const_spec((1, D_MODEL)), _const_spec(w_mla.shape),
                  _const_spec((1, Q_LORA)), _const_spec((Q_LORA, nq)),
                  _const_spec((1, KV_LORA)), _const_spec(w_kv.shape), row(128), row(128)],
        out_specs=[row(D_MODEL), row(MLA_HEADS * ROPE), row(D_MODEL), row(D_MODEL), row(128)],
        out_shape=[jax.ShapeDtypeStruct((T_ALL, D_MODEL), bf16),
                   jax.ShapeDtypeStruct((T_ALL, MLA_HEADS * ROPE), bf16),
                   jax.ShapeDtypeStruct((T_ALL, D_MODEL), bf16),
                   jax.ShapeDtypeStruct((T_ALL, D_MODEL), bf16),
                   jax.ShapeDtypeStruct((T_ALL, 128), bf16)],
        compiler_params=pltpu.CompilerParams(dimension_semantics=("parallel",),
                                             vmem_limit_bytes=VMEM_LIMIT),
        name="mla_front",
    )(xp, xs, g, w_mla, qg, w_q, kvg, w_kv, cos, sin)


def _norm_matmul_kernel(xp_ref, xs_ref, g_ref, w_ref, o_ref, h_ref, *, sigmoid):
    @pl.when(pl.program_id(1) == 0)
    def _():
        h_ref[...] = _rms(_pick_x(xp_ref, xs_ref), g_ref[...]).astype(bf16)

    acc = _dot(h_ref[...], w_ref[...])
    if sigmoid:
        acc = jax.nn.sigmoid(acc)
    o_ref[...] = acc.astype(o_ref.dtype)


def _norm_matmul(xp, xs, g, w, tn, out_dtype, sigmoid, name, tm=512):
    n = w.shape[1]
    return pl.pallas_call(
        functools.partial(_norm_matmul_kernel, sigmoid=sigmoid),
        grid=(T_ALL // tm, n // tn),
        in_specs=_x_specs(tm) + [_const_spec((1, D_MODEL)),
                                 pl.BlockSpec((D_MODEL, tn), lambda i, j: (0, j))],
        out_specs=pl.BlockSpec((tm, tn), lambda i, j: (i, j)),
        out_shape=jax.ShapeDtypeStruct((T_ALL, n), out_dtype),
        scratch_shapes=[pltpu.VMEM((tm, D_MODEL), bf16)],
        compiler_params=pltpu.CompilerParams(dimension_semantics=("parallel", "arbitrary"),
                                             vmem_limit_bytes=VMEM_LIMIT),
        name=name,
    )(xp, xs, g, w)


def _attn_kernel(qn_ref, qr_ref, kn_ref, kr_ref, v_ref, o_ref, kcat_ref, vt_ref, s_ref, p_ref, *,
                 seq, tq, tk, unroll):
    head = pl.program_id(1)
    nsteps = seq // tk

    @pl.when(pl.program_id(2) == 0)
    def _():
        kcat_ref[:, :NOPE] = kn_ref[...]
        kcat_ref[:, NOPE:] = kr_ref[...]
        for c in range(seq // 512):
            cs = slice(c * 512, (c + 1) * 512)
            vt_ref[:, cs] = jnp.transpose(v_ref[cs, :].astype(f32)).astype(bf16)

    lane = lax.broadcasted_iota(jnp.int32, (tq, 128), 1)
    keep = (lane // ROPE) == (head % 2)
    qr = jnp.where(keep, qr_ref[...].astype(f32), 0.0).astype(bf16)
    q = jnp.concatenate([qn_ref[...], qr], axis=1)

    def scores(j, slot):
        rows = pl.ds(pl.multiple_of(j * tk, tk), tk)
        s_ref[slot] = _dot_nt(kcat_ref[rows, :], q)

    def absorb(j, slot, carry):
        m, l8, acc = carry
        s = s_ref[slot]
        m_loc = jnp.max(s.reshape(tk // 8, 8, tq), axis=0)
        m_new = jnp.maximum(m, jnp.max(m_loc, axis=0, keepdims=True))
        alpha = jnp.exp2(m - m_new)
        l8 = alpha * l8
        rc = 256
        for c in range(tk // rc):
            p = jnp.exp2(s_ref[slot, c * rc:(c + 1) * rc, :] - m_new)
            l8 = l8 + jnp.sum(p.reshape(rc // 8, 8, tq), axis=0)
            p_ref[c * rc:(c + 1) * rc, :] = p.astype(bf16)
        cols = pl.ds(pl.multiple_of(j * tk, tk), tk)
        acc = alpha * acc + _dot(vt_ref[:, cols], p_ref[...])
        return m_new, l8, acc

    def run(j0, carry, is_tail):
        for u in range(unroll):
            if not (is_tail and u == unroll - 1):
                scores(j0 + u + 1, (u + 1) % 2)
            carry = absorb(j0 + u, u % 2, carry)
        return carry

    carry = (jnp.full((1, tq), -jnp.inf, f32), jnp.zeros((8, tq), f32), jnp.zeros((V_DIM, tq), f32))
    scores(0, 0)
    carry = lax.fori_loop(0, nsteps // unroll - 1, lambda i, c: run(i * unroll, c, False), carry)
    _, l8, acc = run(nsteps - unroll, carry, True)
    out_t = acc / jnp.sum(l8, axis=0, keepdims=True)
    o_ref[...] = jnp.transpose(out_t).astype(o_ref.dtype)


def _attn_kernel_into(qn_ref, qr_ref, kn_ref, kr_ref, v_ref, prev_ref, o_ref, *scratch, **static):
    del prev_ref
    _attn_kernel(qn_ref, qr_ref, kn_ref, kr_ref, v_ref, o_ref, *scratch, **static)


def _attention(qn, qr, kn, kr, v, row_off, seq, nb, prev=None, tq=256, tk=512, unroll=4):
    qoff = row_off // tq
    soff = row_off // seq
    nq = seq // tq
    static = dict(seq=seq, tq=tq, tk=tk, unroll=unroll)
    in_specs = [pl.BlockSpec((tq, NOPE), lambda b, h, i: (qoff + b * nq + i, h)),
                pl.BlockSpec((tq, 128), lambda b, h, i: (qoff + b * nq + i, h // 2)),
                pl.BlockSpec((seq, NOPE), lambda b, h, i: (soff + b, h)),
                pl.BlockSpec((seq, 128), lambda b, h, i: (soff + b, 0)),
                pl.BlockSpec((seq, V_DIM), lambda b, h, i: (soff + b, h))]
    args = [qn, qr, kn, kr, v]
    if prev is None:
        body, aliases = functools.partial(_attn_kernel, **static), {}
    else:
        body, aliases = functools.partial(_attn_kernel_into, **static), {len(args): 0}
        in_specs.append(pl.BlockSpec(memory_space=pl.ANY))
        args.append(prev)
    return pl.pallas_call(
        body,
        grid=(nb, MLA_HEADS, nq),
        in_specs=in_specs,
        out_specs=pl.BlockSpec((tq, V_DIM), lambda b, h, i: (qoff + b * nq + i, h)),
        out_shape=jax.ShapeDtypeStruct((T_ALL, D_MODEL), bf16),
        input_output_aliases=aliases,
        scratch_shapes=[pltpu.VMEM((seq, 2 * NOPE), bf16), pltpu.VMEM((V_DIM, seq), bf16),
                        pltpu.VMEM((2, tk, tq), f32), pltpu.VMEM((tk, tq), bf16)],
        compiler_params=pltpu.CompilerParams(
            dimension_semantics=("parallel", "parallel", "arbitrary"),
            vmem_limit_bytes=VMEM_LIMIT),
        name=f"mla_attention_s{seq}",
    )(*args)


def _seg_sum(x, bd):
    cols = []
    for c in range(x.shape[1] // GROUP):
        cols.append(_split_dot(x[:, c * GROUP:(c + 1) * GROUP], bd))
    return jnp.concatenate(cols, axis=1)


def _store_groups(ref, x):
    for g in range(N_GROUPS):
        ref[g] = x[:, g * GROUP:(g + 1) * GROUP]


def _rwkv_prep_kernel(u_ref, up_ref, un_ref, mup_ref, mun_ref,
                      w0f_ref, w2f_ref, w0b_ref, w2b_ref, a0f_ref, a2f_ref, a0b_ref, a2b_ref,
                      g2_ref, kkw_ref, kaw_ref, rk_ref, bd_ref,
                      r_ref, v_ref, kk_ref, kf_ref, kb_ref, bf_ref, bb_ref, lwf_ref, lwb_ref,
                      g_ref, bonus_ref, *, tm):
    t0 = pl.program_id(0) * tm
    t1 = t0 + tm
    keep_prev = jnp.where(_is_sequence_edge(t0), 0.0, 1.0)
    keep_next = jnp.where(_is_sequence_edge(t1), 0.0, 1.0)
    row = lax.broadcasted_iota(jnp.int32, (tm, 1), 0)

    def shifted(lo, hi):
        u = u_ref[:, lo:hi]
        prev_row = up_ref[7:8, lo:hi] * keep_prev
        next_row = un_ref[0:1, lo:hi] * keep_next
        prev = jnp.where(row == 0, prev_row, pltpu.roll(u, 1, axis=0))
        nxt = jnp.where(row == tm - 1, next_row, pltpu.roll(u, tm - 1, axis=0))
        return u + mup_ref[:, lo:hi] * (prev - u) + mun_ref[:, lo:hi] * (nxt - u)

    d = D_MODEL
    r = shifted(0, d)
    k = shifted(d, 2 * d)
    v = shifted(2 * d, 3 * d)
    small = shifted(3 * d, U_PACKED)
    xw_f = small[:, 0:128]
    xw_b = small[:, 128:256]
    xa_f = small[:, 256:384]
    xa_b = small[:, 384:512]
    xg = small[:, 512:768]

    def neg_exp_logw(xw, w0_ref, w2_ref):
        z = w0_ref[...] + _dot(jnp.tanh(xw).astype(bf16), w2_ref[...])
        return (-np.exp(-0.5)) / (1.0 + jnp.exp(-z))

    _store_groups(lwf_ref, neg_exp_logw(xw_f, w0f_ref, w2f_ref))
    _store_groups(lwb_ref, neg_exp_logw(xw_b, w0b_ref, w2b_ref))
    a_f = jax.nn.sigmoid(a0f_ref[...] + _dot(xa_f.astype(bf16), a2f_ref[...]))
    a_b = jax.nn.sigmoid(a0b_ref[...] + _dot(xa_b.astype(bf16), a2b_ref[...]))
    g_ref[...] = _dot(jax.nn.sigmoid(xg).astype(bf16), g2_ref[...]).astype(bf16)

    bd = bd_ref[...]
    kk = k * kkw_ref[...]
    norm = jnp.sqrt(_seg_sum(kk * kk, bd))
    kk = kk / jnp.maximum(norm, 1e-12)
    ka = kaw_ref[...]
    k_f = k * (1.0 + (a_f - 1.0) * ka)
    k_b = k * (1.0 + (a_b - 1.0) * ka)
    bonus = _seg_sum(r * (k_f + k_b) * rk_ref[...], bd) * v

    _store_groups(r_ref, r.astype(bf16))
    _store_groups(v_ref, v.astype(bf16))
    _store_groups(kk_ref, kk.astype(bf16))
    _store_groups(kf_ref, k_f.astype(bf16))
    _store_groups(kb_ref, k_b.astype(bf16))
    _store_groups(bf_ref, (kk * a_f).astype(bf16))
    _store_groups(bb_ref, (kk * a_b).astype(bf16))
    bonus_ref[...] = bonus.astype(bf16)


def _rwkv_prep(u, mup, mun, vecs, loras, g2, bd, tm=128):
    hb = tm // 8
    nblk8 = T_ALL // 8
    vec = _const_spec((1, D_MODEL))
    lora = _const_spec((LORA_PAD, D_MODEL))
    w0f, w0b, a0f, a0b, kkw, kaw, rk = vecs
    w2f, w2b, a2f, a2b = loras
    out_bf = jax.ShapeDtypeStruct((T_ALL, D_MODEL), bf16)
    grp_bf = jax.ShapeDtypeStruct((N_GROUPS, T_ALL, GROUP), bf16)
    grp_f = jax.ShapeDtypeStruct((N_GROUPS, T_ALL, GROUP), f32)
    row = pl.BlockSpec((tm, D_MODEL), lambda i: (i, 0))
    grp = pl.BlockSpec((N_GROUPS, tm, GROUP), lambda i: (0, i, 0))
    return pl.pallas_call(
        functools.partial(_rwkv_prep_kernel, tm=tm),
        grid=(T_ALL // tm,),
        in_specs=[pl.BlockSpec((tm, U_PACKED), lambda i: (i, 0)),
                  pl.BlockSpec((8, U_PACKED), lambda i: (jnp.maximum(i * hb - 1, 0), 0)),
                  pl.BlockSpec((8, U_PACKED), lambda i: (jnp.minimum((i + 1) * hb, nblk8 - 1), 0)),
                  _const_spec((1, U_PACKED)), _const_spec((1, U_PACKED)),
                  vec, lora, vec, lora, vec, lora, vec, lora,
                  _const_spec((GATE_LORA, D_MODEL)), vec, vec, vec, _const_spec((GROUP, GROUP))],
        out_specs=[grp] * 9 + [row, row],
        out_shape=[grp_bf] * 7 + [grp_f, grp_f, out_bf, out_bf],
        compiler_params=pltpu.CompilerParams(dimension_semantics=("parallel",),
                                             vmem_limit_bytes=VMEM_LIMIT),
        name="rwkv_prep",
    )(u, u, u, mup, mun, w0f, w2f, w0b, w2b, a0f, a2f, a0b, a2b, g2, kkw, kaw, rk, bd)


def _wkv_masks(reverse):
    C = CHUNK
    t_i = lax.broadcasted_iota(jnp.int32, (C, GROUP), 0)
    s_i = lax.broadcasted_iota(jnp.int32, (C, GROUP), 1) % C
    strict = (s_i > t_i) if reverse else (s_i < t_i)
    incl = (s_i >= t_i) if reverse else (s_i <= t_i)
    eye = jnp.where(s_i == t_i, 1.0, 0.0)
    tt = lax.broadcasted_iota(jnp.int32, (C, C), 0)
    ss = lax.broadcasted_iota(jnp.int32, (C, C), 1)
    tri = jnp.where((ss >= tt) if reverse else (ss <= tt), 1.0, 0.0).astype(bf16)
    return strict, incl, eye, tri


def _wkv_chunk_operators(chains, bdmask):
    C = CHUNK
    n = range(len(chains))
    rs, ks, vs, kks, bs, lws, masks, revs = zip(*chains)
    strict = [m[0] for m in masks]
    incl = [m[1] for m in masks]
    eye = [m[2] for m in masks]
    tri = [m[3] for m in masks]
    last = [0 if rev else C - 1 for rev in revs]

    def bd(x):
        return jnp.where(bdmask, jnp.concatenate([x] * 4, axis=0), 0.0).astype(bf16)

    def ss_mul(x, y_bd):
        return _dot(x.astype(bf16), y_bd)

    lw_hi = [lws[i].astype(bf16) for i in n]
    lw_lo = [(lws[i] - lw_hi[i].astype(f32)).astype(bf16) for i in n]
    cum = [_dot(tri[i], lw_hi[i]) + _dot(tri[i], lw_lo[i]) for i in n]
    w_inv = [jnp.exp(-cum[i]) for i in n]
    w_rem = [jnp.exp(cum[i][last[i]:last[i] + 1, :] - cum[i]) for i in n]
    a_t = [-kks[i] * jnp.exp(cum[i] - lws[i]) for i in n]
    r_t = [rs[i] * jnp.exp(cum[i]) for i in n]

    lhs = [jnp.concatenate([a_t[i], r_t[i]], axis=0).astype(bf16) for i in n]
    ab = [_dot_nt(lhs[i], bd(bs[i] * w_inv[i])) for i in n]
    ak = [_dot_nt(lhs[i], bd(ks[i] * w_inv[i])) for i in n]
    a_ab = [jnp.where(strict[i], ab[i][:C], 0.0) for i in n]
    a_rb = [jnp.where(incl[i], ab[i][C:], 0.0) for i in n]
    a_k = [jnp.concatenate([jnp.where(strict[i], ak[i][:C], 0.0),
                            jnp.where(incl[i], ak[i][C:], 0.0)], axis=0) for i in n]

    tinv = [eye[i] + a_ab[i] for i in n]
    pw = [ss_mul(a_ab[i], bd(a_ab[i])) for i in n]
    for _ in range(4):
        both = [ss_mul(jnp.concatenate([tinv[i], pw[i]], axis=0), bd(pw[i])) for i in n]
        tinv = [tinv[i] + both[i][:C] for i in n]
        pw = [both[i][C:] for i in n]
    tinv = [tinv[i] + ss_mul(tinv[i], bd(pw[i])) for i in n]

    zr = [ss_mul(a_k[i], bd(vs[i])) for i in n]
    pq = [ss_mul(tinv[i], jnp.concatenate([bd(a_t[i]), bd(zr[i][:C])], axis=1)) for i in n]
    p = [pq[i][:, :GROUP] for i in n]
    q = [pq[i][:, GROUP:] for i in n]
    yy = [ss_mul(a_rb[i], jnp.concatenate([bd(p[i]), bd(q[i])], axis=1)) for i in n]
    y_op = [r_t[i] + yy[i][:, :GROUP] for i in n]
    y_c = [yy[i][:, GROUP:] + zr[i][C:] for i in n]
    py = [jnp.concatenate([p[i], y_op[i]], axis=0).astype(bf16) for i in n]
    bkt = [jnp.transpose(jnp.concatenate([bs[i] * w_rem[i], ks[i] * w_rem[i]], axis=0)).astype(bf16)
           for i in n]
    w_tot = [jnp.broadcast_to(jnp.exp(jnp.transpose(cum[i])[:, last[i]:last[i] + 1]), (GROUP, 128))
             for i in n]
    return [(py[i], q[i], y_c[i], bkt[i], w_tot[i]) for i in n]


def _wkv_kernel(rf_ref, vf_ref, kkf_ref, kf_ref, bf_ref, lwf_ref,
                rb_ref, vb_ref, kkb_ref, kb_ref, bb_ref, lwb_ref,
                yf_ref, yb_ref,
                h_ref, py_ref, q_ref, yc_ref, bkt_ref, wt_ref, *, cb, ng):
    C = CHUNK

    c = pl.program_id(1)
    for d, edge in enumerate((c * (cb * C), (pl.num_programs(1) - c) * (cb * C))):
        @pl.when(_is_sequence_edge(edge))
        def _():
            for g in range(ng):
                h_ref[g, d] = jnp.zeros((GROUP, GROUP), f32)

    br = lax.broadcasted_iota(jnp.int32, (GROUP, GROUP), 0) // RWKV_HEAD
    bc = lax.broadcasted_iota(jnp.int32, (GROUP, GROUP), 1) // RWKV_HEAD
    bdmask = br == bc
    dirs = ((rf_ref, vf_ref, kkf_ref, kf_ref, bf_ref, lwf_ref, yf_ref, _wkv_masks(False)),
            (rb_ref, vb_ref, kkb_ref, kb_ref, bb_ref, lwb_ref, yb_ref, _wkv_masks(True)))

    def operators(g, carry):
        chains, slots = [], []
        for j in range(cb):
            rows = pl.ds(j * C, C)
            for d, (r_ref, v_ref, kk_ref, k_ref, b_ref, lw_ref, _, masks) in enumerate(dirs):
                chains.append((r_ref[g, rows, :].astype(f32), k_ref[g, rows, :].astype(f32),
                               v_ref[g, rows, :].astype(f32), kk_ref[g, rows, :].astype(f32),
                               b_ref[g, rows, :].astype(f32), lw_ref[g, rows, :], masks, d == 1))
                slots.append((d, j))
        for (d, j), (py, q, y_c, bkt, wt) in zip(slots, _wkv_chunk_operators(chains, bdmask)):
            py_ref[g, d, j] = py
            q_ref[g, d, j] = q
            yc_ref[g, d, j] = y_c
            bkt_ref[g, d, j] = bkt
            wt_ref[g, d, j] = wt
        return carry

    lax.fori_loop(0, ng, operators, 0)

    def recur(j, carry):
        jd = (j, cb - 1 - j)
        rows = [pl.ds(pl.multiple_of(jd[d] * C, C), C) for d in range(2)]
        gd = [(g, d) for g in range(ng) for d in range(2)]
        h = [h_ref[g, d] for g, d in gd]
        uy = [_dot(py_ref[g, d, jd[d]], h[i].astype(bf16)) for i, (g, d) in enumerate(gd)]
        uv = []
        for i, (g, d) in enumerate(gd):
            dirs[d][6][g, rows[d], :] = uy[i][C:] + yc_ref[g, d, jd[d]]
            u = uy[i][:C] + q_ref[g, d, jd[d]]
            uv.append(jnp.concatenate([u.astype(bf16), dirs[d][1][g, rows[d], :]], axis=0))
        upd = [_dot(bkt_ref[g, d, jd[d]], uv[i]) for i, (g, d) in enumerate(gd)]
        for i, (g, d) in enumerate(gd):
            wt = wt_ref[g, d, jd[d]]
            h_ref[g, d] = h[i] * jnp.concatenate([wt, wt], axis=1) + jnp.where(bdmask, upd[i], 0.0)
        return carry

    lax.fori_loop(0, cb, recur, 0)


def _wkv_scan(r, v, kk, k_f, b_f, lw_f, k_b, b_b, lw_b, cb=4, ng=4):
    C = CHUNK
    rb = cb * C
    nblk = T_ALL // rb
    fwd = pl.BlockSpec((ng, rb, GROUP), lambda g, c: (g, c, 0))
    bwd = pl.BlockSpec((ng, rb, GROUP), lambda g, c: (g, nblk - 1 - c, 0))
    out = jax.ShapeDtypeStruct((N_GROUPS, T_ALL, GROUP), f32)
    return pl.pallas_call(
        functools.partial(_wkv_kernel, cb=cb, ng=ng),
        grid=(N_GROUPS // ng, nblk),
        in_specs=[fwd] * 6 + [bwd] * 6,
        out_specs=[fwd, bwd],
        out_shape=[out, out],
        scratch_shapes=[pltpu.VMEM((ng, 2, GROUP, GROUP), f32),
                        pltpu.VMEM((ng, 2, cb, 2 * C, GROUP), bf16),
                        pltpu.VMEM((ng, 2, cb, C, GROUP), f32),
                        pltpu.VMEM((ng, 2, cb, C, GROUP), f32),
                        pltpu.VMEM((ng, 2, cb, GROUP, 2 * C), bf16),
                        pltpu.VMEM((ng, 2, cb, GROUP, 128), f32)],
        compiler_params=pltpu.CompilerParams(
            dimension_semantics=("parallel", "arbitrary"),
            vmem_limit_bytes=VMEM_LIMIT),
        name="wkv_scan",
    )(r, v, kk, k_f, b_f, lw_f, r, v, kk, k_b, b_b, lw_b)


def _branch_kernel(ya_ref, yf_ref, yb_ref, bonus_ref, g_ref, g1_ref, g2_ref, lnw_ref, lnb_ref,
                   bd_ref, wa_ref, wb_ref, o_ref):
    bd = bd_ref[...]
    y = jnp.concatenate([yf_ref[g] + yb_ref[g] for g in range(N_GROUPS)], axis=1)
    inv_n = 1.0 / RWKV_HEAD
    mean = _seg_sum(y, bd) * inv_n
    dlt = y - mean
    var = _seg_sum(dlt * dlt, bd) * inv_n
    yn = dlt * lax.rsqrt(var + GN_EPS) * lnw_ref[...] + lnb_ref[...]
    yb = ((yn + bonus_ref[...].astype(f32)) * g_ref[...].astype(f32)).astype(bf16)
    ya = ya_ref[...]
    cw = 512
    for c in range(D_MODEL // cw):
        cs = slice(c * cw, (c + 1) * cw)
        oa = _dot(ya, wa_ref[:, cs])
        ob = _dot(yb, wb_ref[:, cs])
        o_ref[:, cs] = (g1_ref[:, cs].astype(f32) * oa + g2_ref[:, cs].astype(f32) * ob).astype(bf16)


def _branch(ya, yf, yb, bonus, g, gates, lnw, lnb, bd, wa, wb, tm=256):
    row = pl.BlockSpec((tm, D_MODEL), lambda i: (i, 0))
    grp = pl.BlockSpec((N_GROUPS, tm, GROUP), lambda i: (0, i, 0))
    return pl.pallas_call(
        _branch_kernel,
        grid=(T_ALL // tm,),
        in_specs=[row, grp, grp, row, row,
                  pl.BlockSpec((tm, D_MODEL), lambda i: (i, 0)),
                  pl.BlockSpec((tm, D_MODEL), lambda i: (i, 1)),
                  _const_spec((1, D_MODEL)), _const_spec((1, D_MODEL)), _const_spec((GROUP, GROUP)),
                  _const_spec((D_MODEL, D_MODEL)), _const_spec((D_MODEL, D_MODEL))],
        out_specs=row,
        out_shape=jax.ShapeDtypeStruct((T_ALL, D_MODEL), bf16),
        compiler_params=pltpu.CompilerParams(dimension_semantics=("parallel",),
                                             vmem_limit_bytes=VMEM_LIMIT),
        name="branch_mix",
    )(ya, yf, yb, bonus, g, gates, gates, lnw, lnb, bd, wa, wb)


def _out_proj_kernel(o_ref, xp_ref, xs_ref, w_ref, g_ref, x1_ref):
    mix = _dot(o_ref[...], w_ref[...])
    x1_ref[...] = _pick_x(xp_ref, xs_ref) + _rms(mix, g_ref[...])


def _out_proj(o, xp, xs, w, g, tm=512):
    row = pl.BlockSpec((tm, D_MODEL), lambda i: (i, 0))
    return pl.pallas_call(
        _out_proj_kernel,
        grid=(T_ALL // tm,),
        in_specs=[row] + _x_specs(tm) + [_const_spec((D_MODEL, D_MODEL)), _const_spec((1, D_MODEL))],
        out_specs=row,
        out_shape=jax.ShapeDtypeStruct((T_ALL, D_MODEL), f32),
        compiler_params=pltpu.CompilerParams(dimension_semantics=("parallel",),
                                             vmem_limit_bytes=VMEM_LIMIT),
        name="out_proj",
    )(o, xp, xs, w, g)


def _mlp_kernel(x_ref, gpre_ref, wu_ref, wd_ref, gpost_ref, op_ref, os_ref, xn_ref, acc_ref):
    i = pl.program_id(0)
    j = pl.program_id(1)
    n_prompt_tiles = T_PROMPT // x_ref.shape[0]

    @pl.when(j == 0)
    def _():
        xn_ref[...] = _rms(x_ref[...], gpre_ref[...]).astype(bf16)
        acc_ref[...] = jnp.zeros_like(acc_ref)

    h = jnp.maximum(_dot(xn_ref[...], wu_ref[...]), 0.0)
    acc_ref[...] += _dot((h * h).astype(bf16), wd_ref[...])

    last = j == pl.num_programs(1) - 1

    @pl.when(jnp.logical_and(last, i < n_prompt_tiles))
    def _():
        op_ref[...] = x_ref[...] + _rms(acc_ref[...], gpost_ref[...])

    @pl.when(jnp.logical_and(last, i >= n_prompt_tiles))
    def _():
        os_ref[...] = x_ref[...] + _rms(acc_ref[...], gpost_ref[...])


def _mlp(x1, gpre, w_up, w_down, gpost, tm=512, tf=1024):
    npb = T_PROMPT // tm
    row = pl.BlockSpec((tm, D_MODEL), lambda i, j: (i, 0))
    out = jax.ShapeDtypeStruct((T_PROMPT, D_MODEL), f32)
    return pl.pallas_call(
        _mlp_kernel,
        grid=(T_ALL // tm, D_FF // tf),
        in_specs=[row, _const_spec((1, D_MODEL)),
                  pl.BlockSpec((D_MODEL, tf), lambda i, j: (0, j)),
                  pl.BlockSpec((tf, D_MODEL), lambda i, j: (j, 0)),
                  _const_spec((1, D_MODEL))],
        out_specs=[pl.BlockSpec((tm, D_MODEL), lambda i, j: (jnp.minimum(i, npb - 1), 0)),
                   pl.BlockSpec((tm, D_MODEL), lambda i, j: (jnp.maximum(i - npb, 0), 0))],
        out_shape=[out, out],
        scratch_shapes=[pltpu.VMEM((tm, D_MODEL), bf16), pltpu.VMEM((tm, D_MODEL), f32)],
        compiler_params=pltpu.CompilerParams(dimension_semantics=("arbitrary", "arbitrary"),
                                             vmem_limit_bytes=VMEM_LIMIT),
        name="mlp",
    )(x1, gpre, w_up, w_down, gpost)


def _pad_lanes(a, width):
    return jnp.pad(a, [(0, 0)] * (a.ndim - 1) + [(0, width - a.shape[-1])])


def _pack_u(a):
    d = D_MODEL
    cuts = [3 * d, 3 * d + 96, 3 * d + 192, 3 * d + 288, 3 * d + 384]
    return jnp.concatenate(
        [a[..., :cuts[0]]]
        + [_pad_lanes(a[..., cuts[i]:cuts[i + 1]], LORA_PAD) for i in range(4)]
        + [a[..., cuts[4]:]], axis=-1)


def _pad_rows(w):
    return jnp.pad(w, [(0, LORA_PAD - w.shape[0]), (0, 0)])


def _rope_tables():
    inv = 1.0 / (ROPE_THETA ** (jnp.arange(0, ROPE, 2, dtype=f32) / ROPE))
    pos = jnp.concatenate([jnp.arange(T_PROMPT, dtype=f32),
                           jnp.tile(jnp.arange(S_SAMPLE, dtype=f32), B_SAMPLE)])
    ang = pos[:, None] * inv[None, :]
    cos, sin = jnp.cos(ang), jnp.sin(ang)
    cos = jnp.concatenate([cos, cos] * 2, axis=1)
    sin = jnp.concatenate([-sin, sin] * 2, axis=1)
    return cos, sin


def kernel(x_prompt, x_sample, norm_pre_mix, w_in, mu_prev, mu_next, mla_q_norm, mla_w_uq, mla_kv_norm, mla_w_ukv, rwkv_w0_f, rwkv_w2_f, rwkv_w0_b, rwkv_w2_b, rwkv_a0_f, rwkv_a2_f, rwkv_a0_b, rwkv_a2_b, rwkv_g2, rwkv_k_k, rwkv_k_a, rwkv_r_k, rwkv_ln_w, rwkv_ln_b, w_branch, w_out, norm_post_mix, norm_pre_mlp, w_mlp_up, w_mlp_down, norm_post_mlp):
    xp = x_prompt.reshape(T_PROMPT, D_MODEL)
    xs = x_sample.reshape(B_SAMPLE * S_SAMPLE, D_MODEL)

    swap = np.concatenate([np.arange(ROPE // 2, ROPE), np.arange(ROPE // 2)])
    w = w_in[0]
    mla_in = Q_LORA + KV_LORA + ROPE
    rwkv_in = 3 * D_MODEL + 4 * 96 + GATE_LORA
    w_kr = w[:, Q_LORA + KV_LORA:mla_in]
    w_krs = w_kr[:, swap]
    w_mla = jnp.concatenate([w[:, :Q_LORA + KV_LORA], w_kr, w_kr, w_krs, w_krs], axis=1).astype(bf16)
    w_u = _pack_u(w[:, mla_in:mla_in + rwkv_in]).astype(bf16)
    w_g = w[:, mla_in + rwkv_in:].astype(bf16)
    mup = _pack_u(mu_prev)
    mun = _pack_u(mu_next)

    wq3 = mla_w_uq[0].reshape(Q_LORA, MLA_HEADS, QK_DIM)
    wq_rope = wq3[:, :, NOPE:]
    w_q = jnp.concatenate([wq3[:, :, :NOPE].reshape(Q_LORA, -1),
                           wq_rope.reshape(Q_LORA, -1),
                           wq_rope[:, :, swap].reshape(Q_LORA, -1)], axis=1).astype(bf16)
    wkv3 = mla_w_ukv[0].reshape(KV_LORA, MLA_HEADS, NOPE + V_DIM)
    w_kv = jnp.concatenate([wkv3[:, :, :NOPE].reshape(KV_LORA, -1),
                            wkv3[:, :, NOPE:].reshape(KV_LORA, -1)], axis=1).astype(bf16)
    cos, sin = _rope_tables()

    qn, qr, kn, v, kr = _mla_front(xp, xs, norm_pre_mix, w_mla, mla_q_norm, w_q, mla_kv_norm, w_kv,
                                   cos, sin)
    ya = _attention(qn, qr, kn, kr, v, 0, T_PROMPT, 1, tq=512, tk=1024, unroll=4)
    ya = _attention(qn, qr, kn, kr, v, T_PROMPT, S_SAMPLE, B_SAMPLE, prev=ya, tq=512, tk=1024, unroll=2)

    u = _norm_matmul(xp, xs, norm_pre_mix, w_u, 768, f32, False, "proj_rwkv")
    head_blk = np.arange(GROUP) // RWKV_HEAD
    bd = jnp.asarray(head_blk[:, None] == head_blk[None, :], dtype=bf16)
    vecs = (rwkv_w0_f, rwkv_w0_b, rwkv_a0_f, rwkv_a0_b, rwkv_k_k, rwkv_k_a,
            rwkv_r_k.reshape(1, D_MODEL))
    loras = tuple(_pad_rows(a[0]).astype(bf16) for a in (rwkv_w2_f, rwkv_w2_b, rwkv_a2_f, rwkv_a2_b))
    (r, vv, kk, k_f, k_b, b_f, b_b, lw_f, lw_b, g, bonus) = _rwkv_prep(
        u, mup, mun, vecs, loras, rwkv_g2[0].astype(bf16), bd)
    y_f, y_b = _wkv_scan(r, vv, kk, k_f, b_f, lw_f, k_b, b_b, lw_b)

    gates = _norm_matmul(xp, xs, norm_pre_mix, w_g, 1024, bf16, True, "proj_gates")
    wb = w_branch[0].astype(bf16)
    o = _branch(ya, y_f, y_b, bonus, g, gates, rwkv_ln_w, rwkv_ln_b, bd,
                wb[:D_MODEL], wb[D_MODEL:])
    x1 = _out_proj(o, xp, xs, w_out[0].astype(bf16), norm_post_mix)
    out_p, out_s = _mlp(x1, norm_pre_mlp, w_mlp_up[0].astype(bf16), w_mlp_down[0].astype(bf16),
                        norm_post_mlp)
    return (out_p.reshape(1, T_PROMPT, D_MODEL), out_s.reshape(B_SAMPLE, S_SAMPLE, D_MODEL))
```

```python
import functools

import numpy as np
import jax
import jax.numpy as jnp
from jax import lax
from jax.experimental import pallas as pl
from jax.experimental.pallas import tpu as pltpu

f32 = jnp.float32
bf16 = jnp.bfloat16

D_MODEL = 2048
T_PROMPT = 8192
S_SAMPLE = 2048
B_SAMPLE = 4
T_ALL = T_PROMPT + B_SAMPLE * S_SAMPLE

MLA_HEADS = 16
Q_LORA = 768
KV_LORA = 512
NOPE = 128
ROPE = 64
V_DIM = 128
QK_DIM = NOPE + ROPE
ROPE_THETA = 10000.0
LOG2_E = 1.4426950408889634

RWKV_HEAD = 64
RWKV_HEADS = 32
LORA_PAD = 128
GATE_LORA = 256
U_PACKED = 3 * D_MODEL + 4 * LORA_PAD + GATE_LORA

D_FF = 4 * D_MODEL
NORM_EPS = 1e-6
GN_EPS = 64e-5

CHUNK = 64
GROUP = 256
N_GROUPS = D_MODEL // GROUP
VMEM_LIMIT = 56 * 1024 * 1024


def _rms(x, g):
    return x * lax.rsqrt(jnp.mean(x * x, axis=-1, keepdims=True) + NORM_EPS) * g


def _dot(a, b):
    return jnp.dot(a, b, preferred_element_type=f32)


def _dot_nt(a, b):
    return lax.dot_general(a, b, (((1,), (1,)), ((), ())), preferred_element_type=f32)


def _dot_tn(a, b):
    return lax.dot_general(a, b, (((0,), (0,)), ((), ())), preferred_element_type=f32)


def _split_dot(x, w):
    hi = x.astype(bf16)
    lo = (x - hi.astype(f32)).astype(bf16)
    return _dot(hi, w) + _dot(lo, w)


def _const_spec(shape):
    return pl.BlockSpec(shape, lambda *_: (0,) * len(shape), pipeline_mode=pl.Buffered(1))


def _x_specs(tm):
    npb = T_PROMPT // tm
    return [pl.BlockSpec((tm, D_MODEL), lambda i, *_: (jnp.minimum(i, npb - 1), 0)),
            pl.BlockSpec((tm, D_MODEL), lambda i, *_: (jnp.maximum(i - npb, 0), 0))]


def _is_sequence_edge(row):
    return jnp.logical_or(row == 0, jnp.logical_and(row >= T_PROMPT, (row - T_PROMPT) % S_SAMPLE == 0))


def _pick_x(xp_ref, xs_ref):
    tm = xp_ref.shape[0]
    return jnp.where(pl.program_id(0) < T_PROMPT // tm, xp_ref[...], xs_ref[...])


def _mla_front_kernel(xp_ref, xs_ref, g_ref, wm_ref, qg_ref, wq_ref, kvg_ref, wkv_ref, cos_ref, sin_ref,
                      qn_ref, qr_ref, kn_ref, v_ref, kr_ref, h_ref):
    h = _rms(_pick_x(xp_ref, xs_ref), g_ref[...]).astype(bf16)
    h_ref[...] = h
    pm = _dot(h, wm_ref[...])
    cos = cos_ref[...]
    sin = sin_ref[...]
    kr_ref[...] = (pm[:, 1280:1408] * cos + pm[:, 1408:1536] * sin).astype(bf16)
    qin = _rms(pm[:, :Q_LORA], qg_ref[...]).astype(bf16)
    kvin = _rms(pm[:, Q_LORA:Q_LORA + KV_LORA], kvg_ref[...]).astype(bf16)
    scale = QK_DIM ** -0.5 * LOG2_E
    cw = 512
    for c in range(D_MODEL // cw):
        q = _dot(qin, wq_ref[:, c * cw:(c + 1) * cw])
        qn_ref[:, c * cw:(c + 1) * cw] = (q * scale).astype(bf16)
    cos4 = jnp.concatenate([cos] * (cw // 128), axis=1)
    sin4 = jnp.concatenate([sin] * (cw // 128), axis=1)
    nrope = MLA_HEADS * ROPE
    for c in range(nrope // cw):
        qa = _dot(qin, wq_ref[:, D_MODEL + c * cw:D_MODEL + (c + 1) * cw])
        qb = _dot(qin, wq_ref[:, D_MODEL + nrope + c * cw:D_MODEL + nrope + (c + 1) * cw])
        qr_ref[:, c * cw:(c + 1) * cw] = ((qa * cos4 + qb * sin4) * scale).astype(bf16)
    for c in range(D_MODEL // cw):
        kn_ref[:, c * cw:(c + 1) * cw] = _dot(kvin, wkv_ref[:, c * cw:(c + 1) * cw]).astype(bf16)
        v_ref[:, c * cw:(c + 1) * cw] = _dot(
            kvin, wkv_ref[:, D_MODEL + c * cw:D_MODEL + (c + 1) * cw]).astype(bf16)


def _mla_front(xp, xs, g, w_mla, qg, w_q, kvg, w_kv, cos, sin, tm=512):
    nq = w_q.shape[1]
    row = lambda w: pl.BlockSpec((tm, w), lambda i: (i, 0))
    return pl.pallas_call(
        _mla_front_kernel,
        grid=(T_ALL // tm,),
        in_specs=_x_specs(tm) + [_const_spec((1, D_MODEL)), _const_spec(w_mla.shape),
                  _const_spec((1, Q_LORA)), _const_spec((Q_LORA, nq)),
                  _const_spec((1, KV_LORA)), _const_spec(w_kv.shape), row(128), row(128)],
        out_specs=[row(D_MODEL), row(MLA_HEADS * ROPE), row(D_MODEL), row(D_MODEL), row(128),
                   row(D_MODEL)],
        out_shape=[jax.ShapeDtypeStruct((T_ALL, D_MODEL), bf16),
                   jax.ShapeDtypeStruct((T_ALL, MLA_HEADS * ROPE), bf16),
                   jax.ShapeDtypeStruct((T_ALL, D_MODEL), bf16),
                   jax.ShapeDtypeStruct((T_ALL, D_MODEL), bf16),
                   jax.ShapeDtypeStruct((T_ALL, 128), bf16),
                   jax.ShapeDtypeStruct((T_ALL, D_MODEL), bf16)],
        compiler_params=pltpu.CompilerParams(dimension_semantics=("parallel",),
                                             vmem_limit_bytes=VMEM_LIMIT),
        name="mla_front",
    )(xp, xs, g, w_mla, qg, w_q, kvg, w_kv, cos, sin)


def _proj_kernel(h_ref, w_ref, o_ref, *, sigmoid):
    acc = _dot(h_ref[...], w_ref[...])
    if sigmoid:
        acc = _sigmoid(acc)
    o_ref[...] = acc.astype(o_ref.dtype)


HALO = 16


def _proj_shift_kernel(hp_ref, h_ref, hn_ref, w_ref, mu0_ref, mup_ref, mun_ref, o_ref):
    tm = h_ref.shape[0]
    t0 = pl.program_id(0) * tm
    keep_prev = jnp.where(_is_sequence_edge(t0), 0.0, 1.0)
    keep_next = jnp.where(_is_sequence_edge(t0 + tm), 0.0, 1.0)
    lhs = jnp.concatenate([hp_ref[...], h_ref[...], hn_ref[...]], axis=0)
    ue = _dot(lhs, w_ref[...])
    n = tm + 2 * HALO
    down = pltpu.roll(ue, 1, axis=0)[HALO:HALO + tm]
    up = pltpu.roll(ue, n - 1, axis=0)[HALO:HALO + tm]
    row8 = lax.broadcasted_iota(jnp.int32, (8, 1), 0)
    head = jnp.where(row8 == 0, down[:8] * keep_prev, down[:8])
    tail = jnp.where(row8 == 7, up[tm - 8:] * keep_next, up[tm - 8:])
    prev = jnp.concatenate([head, down[8:]], axis=0)
    nxt = jnp.concatenate([up[:tm - 8], tail], axis=0)
    o_ref[...] = mu0_ref[...] * ue[HALO:HALO + tm] + mup_ref[...] * prev + mun_ref[...] * nxt


def _proj_shift(h, w, mu0, mup, mun, tn, name, tm=1024):
    n = w.shape[1]
    hb = tm // HALO
    nblk = T_ALL // HALO
    col = pl.BlockSpec((1, tn), lambda i, j: (0, j))
    return pl.pallas_call(
        _proj_shift_kernel,
        grid=(T_ALL // tm, n // tn),
        in_specs=[pl.BlockSpec((HALO, D_MODEL), lambda i, j: (jnp.maximum(i * hb - 1, 0), 0)),
                  pl.BlockSpec((tm, D_MODEL), lambda i, j: (i, 0)),
                  pl.BlockSpec((HALO, D_MODEL), lambda i, j: (jnp.minimum((i + 1) * hb, nblk - 1), 0)),
                  pl.BlockSpec((D_MODEL, tn), lambda i, j: (0, j)), col, col, col],
        out_specs=pl.BlockSpec((tm, tn), lambda i, j: (i, j)),
        out_shape=jax.ShapeDtypeStruct((T_ALL, n), f32),
        compiler_params=pltpu.CompilerParams(dimension_semantics=("parallel", "arbitrary"),
                                             vmem_limit_bytes=VMEM_LIMIT),
        name=name,
    )(h, h, h, w, mu0, mup, mun)


def _proj(h, w, tn, out_dtype, sigmoid, name, tm=1024):
    n = w.shape[1]
    return pl.pallas_call(
        functools.partial(_proj_kernel, sigmoid=sigmoid),
        grid=(T_ALL // tm, n // tn),
        in_specs=[pl.BlockSpec((tm, D_MODEL), lambda i, j: (i, 0)),
                  pl.BlockSpec((D_MODEL, tn), lambda i, j: (0, j))],
        out_specs=pl.BlockSpec((tm, tn), lambda i, j: (i, j)),
        out_shape=jax.ShapeDtypeStruct((T_ALL, n), out_dtype),
        compiler_params=pltpu.CompilerParams(dimension_semantics=("parallel", "arbitrary"),
                                             vmem_limit_bytes=VMEM_LIMIT),
        name=name,
    )(h, w)


def _attn_kernel(qn_ref, qr_ref, kn_ref, kr_ref, v_ref, o_ref, kcat_ref, vt_ref, s_ref, p_ref, *,
                 seq, tq, tk, unroll):
    head = pl.program_id(1)
    nsteps = seq // tk

    @pl.when(pl.program_id(2) == 0)
    def _():
        kcat_ref[:, :NOPE] = kn_ref[...]
        kcat_ref[:, NOPE:] = kr_ref[...]
        for c in range(seq // 512):
            cs = slice(c * 512, (c + 1) * 512)
            vt_ref[:, cs] = jnp.transpose(v_ref[cs, :].astype(f32)).astype(bf16)

    lane = lax.broadcasted_iota(jnp.int32, (tq, 128), 1)
    keep = (lane // ROPE) == (head % 2)
    qr = jnp.where(keep, qr_ref[...].astype(f32), 0.0).astype(bf16)
    q = jnp.concatenate([qn_ref[...], qr], axis=1)

    def scores(j, slot):
        rows = pl.ds(pl.multiple_of(j * tk, tk), tk)
        s_ref[slot] = _dot_nt(kcat_ref[rows, :], q)

    def absorb(j, slot, carry):
        m, l8, acc = carry
        s = s_ref[slot]
        m_loc = jnp.max(s.reshape(tk // 8, 8, tq), axis=0)
        m_new = jnp.maximum(m, jnp.max(m_loc, axis=0, keepdims=True))
        alpha = jnp.exp2(m - m_new)
        l8 = alpha * l8
        rc = 256
        for c in range(tk // rc):
            p = jnp.exp2(s_ref[slot, c * rc:(c + 1) * rc, :] - m_new)
            l8 = l8 + jnp.sum(p.reshape(rc // 8, 8, tq), axis=0)
            p_ref[c * rc:(c + 1) * rc, :] = p.astype(bf16)
        cols = pl.ds(pl.multiple_of(j * tk, tk), tk)
        acc = alpha * acc + _dot(vt_ref[:, cols], p_ref[...])
        return m_new, l8, acc

    def run(j0, carry, is_tail):
        for u in range(unroll):
            if not (is_tail and u == unroll - 1):
                scores(j0 + u + 1, (u + 1) % 2)
            carry = absorb(j0 + u, u % 2, carry)
        return carry

    carry = (jnp.full((1, tq), -jnp.inf, f32), jnp.zeros((8, tq), f32), jnp.zeros((V_DIM, tq), f32))
    scores(0, 0)
    carry = lax.fori_loop(0, nsteps // unroll - 1, lambda i, c: run(i * unroll, c, False), carry)
    _, l8, acc = run(nsteps - unroll, carry, True)
    out_t = acc / jnp.sum(l8, axis=0, keepdims=True)
    o_ref[...] = jnp.transpose(out_t).astype(o_ref.dtype)


def _attn_kernel_into(qn_ref, qr_ref, kn_ref, kr_ref, v_ref, prev_ref, o_ref, *scratch, **static):
    del prev_ref
    _attn_kernel(qn_ref, qr_ref, kn_ref, kr_ref, v_ref, o_ref, *scratch, **static)


def _attention(qn, qr, kn, kr, v, row_off, seq, nb, prev=None, tq=256, tk=512, unroll=4):
    qoff = row_off // tq
    soff = row_off // seq
    nq = seq // tq
    static = dict(seq=seq, tq=tq, tk=tk, unroll=unroll)
    in_specs = [pl.BlockSpec((tq, NOPE), lambda b, h, i: (qoff + b * nq + i, h)),
                pl.BlockSpec((tq, 128), lambda b, h, i: (qoff + b * nq + i, h // 2)),
                pl.BlockSpec((seq, NOPE), lambda b, h, i: (soff + b, h)),
                pl.BlockSpec((seq, 128), lambda b, h, i: (soff + b, 0)),
                pl.BlockSpec((seq, V_DIM), lambda b, h, i: (soff + b, h))]
    args = [qn, qr, kn, kr, v]
    if prev is None:
        body, aliases = functools.partial(_attn_kernel, **static), {}
    else:
        body, aliases = functools.partial(_attn_kernel_into, **static), {len(args): 0}
        in_specs.append(pl.BlockSpec(memory_space=pl.ANY))
        args.append(prev)
    return pl.pallas_call(
        body,
        grid=(nb, MLA_HEADS, nq),
        in_specs=in_specs,
        out_specs=pl.BlockSpec((tq, V_DIM), lambda b, h, i: (qoff + b * nq + i, h)),
        out_shape=jax.ShapeDtypeStruct((T_ALL, D_MODEL), bf16),
        input_output_aliases=aliases,
        scratch_shapes=[pltpu.VMEM((seq, 2 * NOPE), bf16), pltpu.VMEM((V_DIM, seq), bf16),
                        pltpu.VMEM((2, tk, tq), f32), pltpu.VMEM((tk, tq), bf16)],
        compiler_params=pltpu.CompilerParams(
            dimension_semantics=("parallel", "parallel", "arbitrary"),
            vmem_limit_bytes=VMEM_LIMIT),
        name=f"mla_attention_s{seq}",
    )(*args)


def _seg_sum(x, bd, split=True):
    cols = []
    for c in range(x.shape[1] // GROUP):
        xc = x[:, c * GROUP:(c + 1) * GROUP]
        cols.append(_split_dot(xc, bd) if split else _dot(xc.astype(bf16), bd))
    return jnp.concatenate(cols, axis=1)


def _store_groups(ref, x):
    for g in range(N_GROUPS):
        ref[g] = x[:, g * GROUP:(g + 1) * GROUP]


def _sigmoid(z):
    return 1.0 / (1.0 + jnp.exp(-z))


def _rwkv_prep_kernel(u_ref,
                      w0f_ref, w2f_ref, w0b_ref, w2b_ref, a0f_ref, a2f_ref, a0b_ref, a2b_ref,
                      g2_ref, kkw_ref, kaw_ref, rk_ref, bd_ref,
                      r_ref, v_ref, kk_ref, kf_ref, kb_ref, bf_ref, bb_ref, lwf_ref, lwb_ref,
                      g_ref, bonus_ref):
    d = D_MODEL
    r = u_ref[:, 0:d]
    k = u_ref[:, d:2 * d]
    v = u_ref[:, 2 * d:3 * d]
    xw_f = u_ref[:, 3 * d:3 * d + 128]
    xw_b = u_ref[:, 3 * d + 128:3 * d + 256]
    xa_f = u_ref[:, 3 * d + 256:3 * d + 384]
    xa_b = u_ref[:, 3 * d + 384:3 * d + 512]
    xg = u_ref[:, 3 * d + 512:U_PACKED]

    def neg_exp_logw(xw, w0_ref, w2_ref):
        z = w0_ref[...] + _dot(jnp.tanh(xw).astype(bf16), w2_ref[...])
        return (-np.exp(-0.5)) * _sigmoid(z)

    _store_groups(lwf_ref, neg_exp_logw(xw_f, w0f_ref, w2f_ref))
    _store_groups(lwb_ref, neg_exp_logw(xw_b, w0b_ref, w2b_ref))
    a_f = _sigmoid(a0f_ref[...] + _dot(xa_f.astype(bf16), a2f_ref[...]))
    a_b = _sigmoid(a0b_ref[...] + _dot(xa_b.astype(bf16), a2b_ref[...]))
    g_ref[...] = _dot(_sigmoid(xg).astype(bf16), g2_ref[...]).astype(bf16)

    bd = bd_ref[...]
    kk = k * kkw_ref[...]
    kk = kk * lax.rsqrt(jnp.maximum(_seg_sum(kk * kk, bd, split=False), 1e-24))
    ka = kaw_ref[...]
    k_f = k * (1.0 + (a_f - 1.0) * ka)
    k_b = k * (1.0 + (a_b - 1.0) * ka)
    bonus = _seg_sum(r * (k_f + k_b) * rk_ref[...], bd, split=False) * v

    _store_groups(r_ref, r.astype(bf16))
    _store_groups(v_ref, v.astype(bf16))
    _store_groups(kk_ref, kk.astype(bf16))
    _store_groups(kf_ref, k_f.astype(bf16))
    _store_groups(kb_ref, k_b.astype(bf16))
    _store_groups(bf_ref, (kk * a_f).astype(bf16))
    _store_groups(bb_ref, (kk * a_b).astype(bf16))
    bonus_ref[...] = bonus.astype(bf16)


def _rwkv_prep(u, vecs, loras, g2, bd, tm=128):
    vec = _const_spec((1, D_MODEL))
    lora = _const_spec((LORA_PAD, D_MODEL))
    w0f, w0b, a0f, a0b, kkw, kaw, rk = vecs
    w2f, w2b, a2f, a2b = loras
    out_bf = jax.ShapeDtypeStruct((T_ALL, D_MODEL), bf16)
    grp_bf = jax.ShapeDtypeStruct((N_GROUPS, T_ALL, GROUP), bf16)
    grp_f = jax.ShapeDtypeStruct((N_GROUPS, T_ALL, GROUP), f32)
    row = pl.BlockSpec((tm, D_MODEL), lambda i: (i, 0))
    grp = pl.BlockSpec((N_GROUPS, tm, GROUP), lambda i: (0, i, 0))
    return pl.pallas_call(
        _rwkv_prep_kernel,
        grid=(T_ALL // tm,),
        in_specs=[pl.BlockSpec((tm, U_PACKED), lambda i: (i, 0)),
                  vec, lora, vec, lora, vec, lora, vec, lora,
                  _const_spec((GATE_LORA, D_MODEL)), vec, vec, vec, _const_spec((GROUP, GROUP))],
        out_specs=[grp] * 9 + [row, row],
        out_shape=[grp_bf] * 7 + [grp_f, grp_f, out_bf, out_bf],
        compiler_params=pltpu.CompilerParams(dimension_semantics=("parallel",),
                                             vmem_limit_bytes=VMEM_LIMIT),
        name="rwkv_prep",
    )(u, w0f, w2f, w0b, w2b, a0f, a2f, a0b, a2b, g2, kkw, kaw, rk, bd)


def _wkv_masks(reverse):
    C = CHUNK
    t_i = lax.broadcasted_iota(jnp.int32, (C, GROUP), 0)
    s_i = lax.broadcasted_iota(jnp.int32, (C, GROUP), 1) % C
    strict = (s_i > t_i) if reverse else (s_i < t_i)
    incl = (s_i >= t_i) if reverse else (s_i <= t_i)
    eye = jnp.where(s_i == t_i, 1.0, 0.0)
    tt = lax.broadcasted_iota(jnp.int32, (C, C), 0)
    ss = lax.broadcasted_iota(jnp.int32, (C, C), 1)
    tri = jnp.where((ss >= tt) if reverse else (ss <= tt), 1.0, 0.0).astype(bf16)
    return strict, incl, eye, tri


def _wkv_chunk_operators(chains, bdmask):
    C = CHUNK
    n = range(len(chains))
    rs, ks, vs, kks, bs, lws, masks, revs = zip(*chains)
    strict = [m[0] for m in masks]
    incl = [m[1] for m in masks]
    eye = [m[2] for m in masks]
    tri = [m[3] for m in masks]
    last = [0 if rev else C - 1 for rev in revs]
    even_blk = (lax.broadcasted_iota(jnp.int32, (2 * C, GROUP), 1) // RWKV_HEAD) % 2 == 0

    def bd(x):
        return jnp.where(bdmask, jnp.concatenate([x] * 4, axis=0), 0.0).astype(bf16)

    def ss_mul(x, y_bd):
        return _dot(x.astype(bf16), y_bd)

    lw_hi = [lws[i].astype(bf16) for i in n]
    lw_lo = [(lws[i] - lw_hi[i].astype(f32)).astype(bf16) for i in n]
    cum = [_dot(tri[i], lw_hi[i]) + _dot(tri[i], lw_lo[i]) for i in n]
    w_inv = [jnp.exp(-cum[i]) for i in n]
    w_rem = [jnp.exp(cum[i][last[i]:last[i] + 1, :] - cum[i]) for i in n]
    a_t = [-kks[i] * jnp.exp(cum[i] - lws[i]) for i in n]
    r_t = [rs[i] * jnp.exp(cum[i]) for i in n]

    lhs = [jnp.concatenate([a_t[i], r_t[i]], axis=0).astype(bf16) for i in n]
    b_i = [bs[i] * w_inv[i] for i in n]
    k_i = [ks[i] * w_inv[i] for i in n]
    bk_t = [jnp.transpose(jnp.concatenate([b_i[i], k_i[i]], axis=0)) for i in n]
    kb_t = [jnp.transpose(jnp.concatenate([k_i[i], b_i[i]], axis=0)) for i in n]
    mix1 = [_dot(lhs[i], jnp.where(bdmask, jnp.concatenate([bk_t[i]] * 2, axis=1), 0.0).astype(bf16))
            for i in n]
    mix2 = [_dot(lhs[i], jnp.where(bdmask, jnp.concatenate([kb_t[i]] * 2, axis=1), 0.0).astype(bf16))
            for i in n]
    ab = [jnp.where(even_blk, mix1[i], mix2[i]) for i in n]
    ak = [jnp.where(even_blk, mix2[i], mix1[i]) for i in n]
    a_ab = [jnp.where(strict[i], ab[i][:C], 0.0) for i in n]
    a_rb = [jnp.where(incl[i], ab[i][C:], 0.0) for i in n]
    a_k = [jnp.concatenate([jnp.where(strict[i], ak[i][:C], 0.0),
                            jnp.where(incl[i], ak[i][C:], 0.0)], axis=0) for i in n]

    tinv = [eye[i] + a_ab[i] for i in n]
    pw = [ss_mul(a_ab[i], bd(a_ab[i])) for i in n]
    for _ in range(4):
        both = [ss_mul(jnp.concatenate([tinv[i], pw[i]], axis=0), bd(pw[i])) for i in n]
        tinv = [tinv[i] + both[i][:C] for i in n]
        pw = [both[i][C:] for i in n]
    tinv = [tinv[i] + ss_mul(tinv[i], bd(pw[i])) for i in n]

    zr = [ss_mul(a_k[i], bd(vs[i])) for i in n]
    pq = [ss_mul(tinv[i], jnp.concatenate([bd(a_t[i]), bd(zr[i][:C])], axis=1)) for i in n]
    p = [pq[i][:, :GROUP] for i in n]
    q = [pq[i][:, GROUP:] for i in n]
    yy = [ss_mul(a_rb[i], jnp.concatenate([bd(p[i]), bd(q[i])], axis=1)) for i in n]
    y_op = [r_t[i] + yy[i][:, :GROUP] for i in n]
    y_c = [yy[i][:, GROUP:] + zr[i][C:] for i in n]
    py = [jnp.concatenate([p[i], y_op[i]], axis=0).astype(bf16) for i in n]
    bkt = [jnp.transpose(jnp.concatenate([bs[i] * w_rem[i], ks[i] * w_rem[i]], axis=0)).astype(bf16)
           for i in n]
    w_tot = [jnp.broadcast_to(jnp.exp(jnp.transpose(cum[i])[:, last[i]:last[i] + 1]), (GROUP, 128))
             for i in n]
    return [(py[i], q[i], y_c[i], bkt[i], w_tot[i]) for i in n]


def _wkv_kernel(rf_ref, vf_ref, kkf_ref, kf_ref, bf_ref, lwf_ref,
                rb_ref, vb_ref, kkb_ref, kb_ref, bb_ref, lwb_ref,
                yf_ref, yb_ref,
                h_ref, py_ref, q_ref, yc_ref, bkt_ref, wt_ref, *, cb, ng):
    C = CHUNK

    c = pl.program_id(1)
    for d, edge in enumerate((c * (cb * C), (pl.num_programs(1) - c) * (cb * C))):
        @pl.when(_is_sequence_edge(edge))
        def _():
            for g in range(ng):
                h_ref[g, d] = jnp.zeros((GROUP, GROUP), f32)

    br = lax.broadcasted_iota(jnp.int32, (GROUP, GROUP), 0) // RWKV_HEAD
    bc = lax.broadcasted_iota(jnp.int32, (GROUP, GROUP), 1) // RWKV_HEAD
    bdmask = br == bc
    dirs = ((rf_ref, vf_ref, kkf_ref, kf_ref, bf_ref, lwf_ref, yf_ref, _wkv_masks(False)),
            (rb_ref, vb_ref, kkb_ref, kb_ref, bb_ref, lwb_ref, yb_ref, _wkv_masks(True)))

    def operators(gp, carry):
        chains, slots = [], []
        for g in (2 * gp, 2 * gp + 1):
            for j in range(cb):
                rows = pl.ds(j * C, C)
                for d, (r_ref, v_ref, kk_ref, k_ref, b_ref, lw_ref, _, masks) in enumerate(dirs):
                    chains.append((r_ref[g, rows, :].astype(f32), k_ref[g, rows, :].astype(f32),
                                   v_ref[g, rows, :].astype(f32), kk_ref[g, rows, :].astype(f32),
                                   b_ref[g, rows, :].astype(f32), lw_ref[g, rows, :], masks, d == 1))
                    slots.append((g, d, j))
        for (g, d, j), (py, q, y_c, bkt, wt) in zip(slots, _wkv_chunk_operators(chains, bdmask)):
            py_ref[g, d, j] = py
            q_ref[g, d, j] = q
            yc_ref[g, d, j] = y_c
            bkt_ref[g, d, j] = bkt
            wt_ref[g, d, j] = wt
        return carry

    lax.fori_loop(0, ng // 2, operators, 0)

    def recur(j, carry):
        jd = (j, cb - 1 - j)
        rows = [pl.ds(pl.multiple_of(jd[d] * C, C), C) for d in range(2)]
        gd = [(g, d) for g in range(ng) for d in range(2)]
        h = [h_ref[g, d] for g, d in gd]
        uy = [_dot(py_ref[g, d, jd[d]], h[i].astype(bf16)) for i, (g, d) in enumerate(gd)]
        uv = []
        for i, (g, d) in enumerate(gd):
            dirs[d][6][g, rows[d], :] = uy[i][C:] + yc_ref[g, d, jd[d]]
            u = uy[i][:C] + q_ref[g, d, jd[d]]
            uv.append(jnp.concatenate([u.astype(bf16), dirs[d][1][g, rows[d], :]], axis=0))
        upd = [_dot(bkt_ref[g, d, jd[d]], uv[i]) for i, (g, d) in enumerate(gd)]
        for i, (g, d) in enumerate(gd):
            wt = wt_ref[g, d, jd[d]]
            h_ref[g, d] = h[i] * jnp.concatenate([wt, wt], axis=1) + jnp.where(bdmask, upd[i], 0.0)
        return carry

    lax.fori_loop(0, cb, recur, 0)


def _wkv_scan(r, v, kk, k_f, b_f, lw_f, k_b, b_b, lw_b, cb=4, ng=4):
    C = CHUNK
    rb = cb * C
    nblk = T_ALL // rb
    fwd = pl.BlockSpec((ng, rb, GROUP), lambda g, c: (g, c, 0))
    bwd = pl.BlockSpec((ng, rb, GROUP), lambda g, c: (g, nblk - 1 - c, 0))
    out = jax.ShapeDtypeStruct((N_GROUPS, T_ALL, GROUP), f32)
    return pl.pallas_call(
        functools.partial(_wkv_kernel, cb=cb, ng=ng),
        grid=(N_GROUPS // ng, nblk),
        in_specs=[fwd] * 6 + [bwd] * 6,
        out_specs=[fwd, bwd],
        out_shape=[out, out],
        scratch_shapes=[pltpu.VMEM((ng, 2, GROUP, GROUP), f32),
                        pltpu.VMEM((ng, 2, cb, 2 * C, GROUP), bf16),
                        pltpu.VMEM((ng, 2, cb, C, GROUP), f32),
                        pltpu.VMEM((ng, 2, cb, C, GROUP), f32),
                        pltpu.VMEM((ng, 2, cb, GROUP, 2 * C), bf16),
                        pltpu.VMEM((ng, 2, cb, GROUP, 128), f32)],
        compiler_params=pltpu.CompilerParams(
            dimension_semantics=("parallel", "arbitrary"),
            vmem_limit_bytes=VMEM_LIMIT),
        name="wkv_scan",
    )(r, v, kk, k_f, b_f, lw_f, r, v, kk, k_b, b_b, lw_b)


def _branch_kernel(ya_ref, yf_ref, yb_ref, bonus_ref, g_ref, g1_ref, g2_ref, lnw_ref, lnb_ref,
                   bd_ref, wa_ref, wb_ref, o_ref):
    bd = bd_ref[...]
    y = jnp.concatenate([yf_ref[g] + yb_ref[g] for g in range(N_GROUPS)], axis=1)
    inv_n = 1.0 / RWKV_HEAD
    mean = _seg_sum(y, bd) * inv_n
    dlt = y - mean
    var = _seg_sum(dlt * dlt, bd) * inv_n
    yn = dlt * lax.rsqrt(var + GN_EPS) * lnw_ref[...] + lnb_ref[...]
    yb = ((yn + bonus_ref[...].astype(f32)) * g_ref[...].astype(f32)).astype(bf16)
    ya = ya_ref[...]
    cw = 512
    for c in range(D_MODEL // cw):
        cs = slice(c * cw, (c + 1) * cw)
        oa = _dot(ya, wa_ref[:, cs])
        ob = _dot(yb, wb_ref[:, cs])
        o_ref[:, cs] = (g1_ref[:, cs].astype(f32) * oa + g2_ref[:, cs].astype(f32) * ob).astype(bf16)


def _branch(ya, yf, yb, bonus, g, gates, lnw, lnb, bd, wa, wb, tm=256):
    row = pl.BlockSpec((tm, D_MODEL), lambda i: (i, 0))
    grp = pl.BlockSpec((N_GROUPS, tm, GROUP), lambda i: (0, i, 0))
    return pl.pallas_call(
        _branch_kernel,
        grid=(T_ALL // tm,),
        in_specs=[row, grp, grp, row, row,
                  pl.BlockSpec((tm, D_MODEL), lambda i: (i, 0)),
                  pl.BlockSpec((tm, D_MODEL), lambda i: (i, 1)),
                  _const_spec((1, D_MODEL)), _const_spec((1, D_MODEL)), _const_spec((GROUP, GROUP)),
                  _const_spec((D_MODEL, D_MODEL)), _const_spec((D_MODEL, D_MODEL))],
        out_specs=row,
        out_shape=jax.ShapeDtypeStruct((T_ALL, D_MODEL), bf16),
        compiler_params=pltpu.CompilerParams(dimension_semantics=("parallel",),
                                             vmem_limit_bytes=VMEM_LIMIT),
        name="branch_mix",
    )(ya, yf, yb, bonus, g, gates, gates, lnw, lnb, bd, wa, wb)


def _out_proj_kernel(o_ref, xp_ref, xs_ref, w_ref, g_ref, x1_ref):
    mix = _dot(o_ref[...], w_ref[...])
    x1_ref[...] = _pick_x(xp_ref, xs_ref) + _rms(mix, g_ref[...])


def _out_proj(o, xp, xs, w, g, tm=512):
    row = pl.BlockSpec((tm, D_MODEL), lambda i: (i, 0))
    return pl.pallas_call(
        _out_proj_kernel,
        grid=(T_ALL // tm,),
        in_specs=[row] + _x_specs(tm) + [_const_spec((D_MODEL, D_MODEL)), _const_spec((1, D_MODEL))],
        out_specs=row,
        out_shape=jax.ShapeDtypeStruct((T_ALL, D_MODEL), f32),
        compiler_params=pltpu.CompilerParams(dimension_semantics=("parallel",),
                                             vmem_limit_bytes=VMEM_LIMIT),
        name="out_proj",
    )(o, xp, xs, w, g)


def _mlp_kernel(x_ref, gpre_ref, wu_ref, wd_ref, gpost_ref, op_ref, os_ref, xn_ref, acc_ref):
    i = pl.program_id(0)
    j = pl.program_id(1)
    n_prompt_tiles = T_PROMPT // x_ref.shape[0]

    @pl.when(j == 0)
    def _():
        xn_ref[...] = _rms(x_ref[...], gpre_ref[...]).astype(bf16)
        acc_ref[...] = jnp.zeros_like(acc_ref)

    h = jnp.maximum(_dot(xn_ref[...], wu_ref[...]), 0.0)
    acc_ref[...] += _dot((h * h).astype(bf16), wd_ref[...])

    last = j == pl.num_programs(1) - 1

    @pl.when(jnp.logical_and(last, i < n_prompt_tiles))
    def _():
        op_ref[...] = x_ref[...] + _rms(acc_ref[...], gpost_ref[...])

    @pl.when(jnp.logical_and(last, i >= n_prompt_tiles))
    def _():
        os_ref[...] = x_ref[...] + _rms(acc_ref[...], gpost_ref[...])


def _mlp(x1, gpre, w_up, w_down, gpost, tm=512, tf=1024):
    npb = T_PROMPT // tm
    row = pl.BlockSpec((tm, D_MODEL), lambda i, j: (i, 0))
    out = jax.ShapeDtypeStruct((T_PROMPT, D_MODEL), f32)
    return pl.pallas_call(
        _mlp_kernel,
        grid=(T_ALL // tm, D_FF // tf),
        in_specs=[row, _const_spec((1, D_MODEL)),
                  pl.BlockSpec((D_MODEL, tf), lambda i, j: (0, j)),
                  pl.BlockSpec((tf, D_MODEL), lambda i, j: (j, 0)),
                  _const_spec((1, D_MODEL))],
        out_specs=[pl.BlockSpec((tm, D_MODEL), lambda i, j: (jnp.minimum(i, npb - 1), 0)),
                   pl.BlockSpec((tm, D_MODEL), lambda i, j: (jnp.maximum(i - npb, 0), 0))],
        out_shape=[out, out],
        scratch_shapes=[pltpu.VMEM((tm, D_MODEL), bf16), pltpu.VMEM((tm, D_MODEL), f32)],
        compiler_params=pltpu.CompilerParams(dimension_semantics=("arbitrary", "arbitrary"),
                                             vmem_limit_bytes=VMEM_LIMIT),
        name="mlp",
    )(x1, gpre, w_up, w_down, gpost)


def _pad_lanes(a, width):
    return jnp.pad(a, [(0, 0)] * (a.ndim - 1) + [(0, width - a.shape[-1])])


def _pack_u(a):
    d = D_MODEL
    cuts = [3 * d, 3 * d + 96, 3 * d + 192, 3 * d + 288, 3 * d + 384]
    return jnp.concatenate(
        [a[..., :cuts[0]]]
        + [_pad_lanes(a[..., cuts[i]:cuts[i + 1]], LORA_PAD) for i in range(4)]
        + [a[..., cuts[4]:]], axis=-1)


def _pad_rows(w):
    return jnp.pad(w, [(0, LORA_PAD - w.shape[0]), (0, 0)])


def _rope_tables():
    inv = 1.0 / (ROPE_THETA ** (jnp.arange(0, ROPE, 2, dtype=f32) / ROPE))
    pos = jnp.concatenate([jnp.arange(T_PROMPT, dtype=f32),
                           jnp.tile(jnp.arange(S_SAMPLE, dtype=f32), B_SAMPLE)])
    ang = pos[:, None] * inv[None, :]
    cos, sin = jnp.cos(ang), jnp.sin(ang)
    cos = jnp.concatenate([cos, cos] * 2, axis=1)
    sin = jnp.concatenate([-sin, sin] * 2, axis=1)
    return cos, sin


def kernel(x_prompt, x_sample, norm_pre_mix, w_in, mu_prev, mu_next, mla_q_norm, mla_w_uq, mla_kv_norm, mla_w_ukv, rwkv_w0_f, rwkv_w2_f, rwkv_w0_b, rwkv_w2_b, rwkv_a0_f, rwkv_a2_f, rwkv_a0_b, rwkv_a2_b, rwkv_g2, rwkv_k_k, rwkv_k_a, rwkv_r_k, rwkv_ln_w, rwkv_ln_b, w_branch, w_out, norm_post_mix, norm_pre_mlp, w_mlp_up, w_mlp_down, norm_post_mlp):
    xp = x_prompt.reshape(T_PROMPT, D_MODEL)
    xs = x_sample.reshape(B_SAMPLE * S_SAMPLE, D_MODEL)

    swap = np.concatenate([np.arange(ROPE // 2, ROPE), np.arange(ROPE // 2)])
    w = w_in[0]
    mla_in = Q_LORA + KV_LORA + ROPE
    rwkv_in = 3 * D_MODEL + 4 * 96 + GATE_LORA
    w_kr = w[:, Q_LORA + KV_LORA:mla_in]
    w_krs = w_kr[:, swap]
    w_mla = jnp.concatenate([w[:, :Q_LORA + KV_LORA], w_kr, w_kr, w_krs, w_krs], axis=1).astype(bf16)
    w_u = _pack_u(w[:, mla_in:mla_in + rwkv_in]).astype(bf16)
    w_g = w[:, mla_in + rwkv_in:].astype(bf16)
    mup = _pack_u(mu_prev)
    mun = _pack_u(mu_next)

    wq3 = mla_w_uq[0].reshape(Q_LORA, MLA_HEADS, QK_DIM)
    wq_rope = wq3[:, :, NOPE:]
    w_q = jnp.concatenate([wq3[:, :, :NOPE].reshape(Q_LORA, -1),
                           wq_rope.reshape(Q_LORA, -1),
                           wq_rope[:, :, swap].reshape(Q_LORA, -1)], axis=1).astype(bf16)
    wkv3 = mla_w_ukv[0].reshape(KV_LORA, MLA_HEADS, NOPE + V_DIM)
    w_kv = jnp.concatenate([wkv3[:, :, :NOPE].reshape(KV_LORA, -1),
                            wkv3[:, :, NOPE:].reshape(KV_LORA, -1)], axis=1).astype(bf16)
    cos, sin = _rope_tables()

    qn, qr, kn, v, kr, h = _mla_front(xp, xs, norm_pre_mix, w_mla, mla_q_norm, w_q, mla_kv_norm, w_kv,
                                      cos, sin)
    ya = _attention(qn, qr, kn, kr, v, 0, T_PROMPT, 1, tq=512, tk=1024, unroll=4)
    ya = _attention(qn, qr, kn, kr, v, T_PROMPT, S_SAMPLE, B_SAMPLE, prev=ya, tq=512, tk=1024, unroll=2)

    u = _proj_shift(h, w_u, 1.0 - mup - mun, mup, mun, 768, "proj_rwkv")
    head_blk = np.arange(GROUP) // RWKV_HEAD
    bd = jnp.asarray(head_blk[:, None] == head_blk[None, :], dtype=bf16)
    vecs = (rwkv_w0_f, rwkv_w0_b, rwkv_a0_f, rwkv_a0_b, rwkv_k_k, rwkv_k_a,
            rwkv_r_k.reshape(1, D_MODEL))
    loras = tuple(_pad_rows(a[0]).astype(bf16) for a in (rwkv_w2_f, rwkv_w2_b, rwkv_a2_f, rwkv_a2_b))
    (r, vv, kk, k_f, k_b, b_f, b_b, lw_f, lw_b, g, bonus) = _rwkv_prep(
        u, vecs, loras, rwkv_g2[0].astype(bf16), bd)
    y_f, y_b = _wkv_scan(r, vv, kk, k_f, b_f, lw_f, k_b, b_b, lw_b)

    gates = _proj(h, w_g, 1024, bf16, True, "proj_gates", tm=2048)
    wb = w_branch[0].astype(bf16)
    o = _branch(ya, y_f, y_b, bonus, g, gates, rwkv_ln_w, rwkv_ln_b, bd,
                wb[:D_MODEL], wb[D_MODEL:])
    x1 = _out_proj(o, xp, xs, w_out[0].astype(bf16), norm_post_mix)
    out_p, out_s = _mlp(x1, norm_pre_mlp, w_mlp_up[0].astype(bf16), w_mlp_down[0].astype(bf16),
                        norm_post_mlp)
    return (out_p.reshape(1, T_PROMPT, D_MODEL), out_s.reshape(B_SAMPLE, S_SAMPLE, D_MODEL))
```

```python
import functools

import numpy as np
import jax
import jax.numpy as jnp
from jax import lax
from jax.experimental import pallas as pl
from jax.experimental.pallas import tpu as pltpu

f32 = jnp.float32
bf16 = jnp.bfloat16

D_MODEL = 2048
T_PROMPT = 8192
S_SAMPLE = 2048
B_SAMPLE = 4
T_ALL = T_PROMPT + B_SAMPLE * S_SAMPLE

MLA_HEADS = 16
Q_LORA = 768
KV_LORA = 512
NOPE = 128
ROPE = 64
V_DIM = 128
QK_DIM = NOPE + ROPE
ROPE_THETA = 10000.0
LOG2_E = 1.4426950408889634

RWKV_HEAD = 64
RWKV_HEADS = 32
LORA_PAD = 128
GATE_LORA = 256
U_PACKED = 3 * D_MODEL + 4 * LORA_PAD + GATE_LORA

D_FF = 4 * D_MODEL
NORM_EPS = 1e-6
GN_EPS = 64e-5

CHUNK = 64
GROUP = 256
N_GROUPS = D_MODEL // GROUP
VMEM_LIMIT = 56 * 1024 * 1024


def _rms(x, g):
    return x * lax.rsqrt(jnp.mean(x * x, axis=-1, keepdims=True) + NORM_EPS) * g


def _dot(a, b):
    return jnp.dot(a, b, preferred_element_type=f32)


def _dot_nt(a, b):
    return lax.dot_general(a, b, (((1,), (1,)), ((), ())), preferred_element_type=f32)


def _dot_tn(a, b):
    return lax.dot_general(a, b, (((0,), (0,)), ((), ())), preferred_element_type=f32)


def _split_dot(x, w):
    hi = x.astype(bf16)
    lo = (x - hi.astype(f32)).astype(bf16)
    return _dot(hi, w) + _dot(lo, w)


def _const_spec(shape):
    return pl.BlockSpec(shape, lambda *_: (0,) * len(shape), pipeline_mode=pl.Buffered(1))


def _x_specs(tm):
    npb = T_PROMPT // tm
    return [pl.BlockSpec((tm, D_MODEL), lambda i, *_: (jnp.minimum(i, npb - 1), 0)),
            pl.BlockSpec((tm, D_MODEL), lambda i, *_: (jnp.maximum(i - npb, 0), 0))]


def _is_sequence_edge(row):
    return jnp.logical_or(row == 0, jnp.logical_and(row >= T_PROMPT, (row - T_PROMPT) % S_SAMPLE == 0))


def _pick_x(xp_ref, xs_ref):
    tm = xp_ref.shape[0]
    return jnp.where(pl.program_id(0) < T_PROMPT // tm, xp_ref[...], xs_ref[...])


def _mla_front_kernel(xp_ref, xs_ref, g_ref, wm_ref, qg_ref, wq_ref, kvg_ref, wkv_ref, cos_ref, sin_ref,
                      qn_ref, qr_ref, kn_ref, v_ref, kr_ref, h_ref):
    h = _rms(_pick_x(xp_ref, xs_ref), g_ref[...]).astype(bf16)
    h_ref[...] = h
    pm = _dot(h, wm_ref[...])
    cos = cos_ref[...]
    sin = sin_ref[...]
    kr_ref[...] = (pm[:, 1280:1408] * cos + pm[:, 1408:1536] * sin).astype(bf16)
    qin = _rms(pm[:, :Q_LORA], qg_ref[...]).astype(bf16)
    kvin = _rms(pm[:, Q_LORA:Q_LORA + KV_LORA], kvg_ref[...]).astype(bf16)
    scale = QK_DIM ** -0.5 * LOG2_E
    cw = 512
    for c in range(D_MODEL // cw):
        q = _dot(qin, wq_ref[:, c * cw:(c + 1) * cw])
        qn_ref[:, c * cw:(c + 1) * cw] = (q * scale).astype(bf16)
    cos4 = jnp.concatenate([cos] * (cw // 128), axis=1)
    sin4 = jnp.concatenate([sin] * (cw // 128), axis=1)
    nrope = MLA_HEADS * ROPE
    for c in range(nrope // cw):
        qa = _dot(qin, wq_ref[:, D_MODEL + c * cw:D_MODEL + (c + 1) * cw])
        qb = _dot(qin, wq_ref[:, D_MODEL + nrope + c * cw:D_MODEL + nrope + (c + 1) * cw])
        qr_ref[:, c * cw:(c + 1) * cw] = ((qa * cos4 + qb * sin4) * scale).astype(bf16)
    for c in range(D_MODEL // cw):
        kn_ref[:, c * cw:(c + 1) * cw] = _dot(kvin, wkv_ref[:, c * cw:(c + 1) * cw]).astype(bf16)
        v_ref[:, c * cw:(c + 1) * cw] = _dot(
            kvin, wkv_ref[:, D_MODEL + c * cw:D_MODEL + (c + 1) * cw]).astype(bf16)


def _mla_front(xp, xs, g, w_mla, qg, w_q, kvg, w_kv, cos, sin, tm=512):
    nq = w_q.shape[1]
    row = lambda w: pl.BlockSpec((tm, w), lambda i: (i, 0))
    return pl.pallas_call(
        _mla_front_kernel,
        grid=(T_ALL // tm,),
        in_specs=_x_specs(tm) + [_const_spec((1, D_MODEL)), _const_spec(w_mla.shape),
                  _const_spec((1, Q_LORA)), _const_spec((Q_LORA, nq)),
                  _const_spec((1, KV_LORA)), _const_spec(w_kv.shape), row(128), row(128)],
        out_specs=[row(D_MODEL), row(MLA_HEADS * ROPE), row(D_MODEL), row(D_MODEL), row(128),
                   row(D_MODEL)],
        out_shape=[jax.ShapeDtypeStruct((T_ALL, D_MODEL), bf16),
                   jax.ShapeDtypeStruct((T_ALL, MLA_HEADS * ROPE), bf16),
                   jax.ShapeDtypeStruct((T_ALL, D_MODEL), bf16),
                   jax.ShapeDtypeStruct((T_ALL, D_MODEL), bf16),
                   jax.ShapeDtypeStruct((T_ALL, 128), bf16),
                   jax.ShapeDtypeStruct((T_ALL, D_MODEL), bf16)],
        compiler_params=pltpu.CompilerParams(dimension_semantics=("parallel",),
                                             vmem_limit_bytes=VMEM_LIMIT),
        name="mla_front",
    )(xp, xs, g, w_mla, qg, w_q, kvg, w_kv, cos, sin)


def _proj_kernel(h_ref, w_ref, o_ref, *, sigmoid):
    acc = _dot(h_ref[...], w_ref[...])
    if sigmoid:
        acc = _sigmoid(acc)
    o_ref[...] = acc.astype(o_ref.dtype)


HALO = 16


def _proj_shift_kernel(hp_ref, h_ref, hn_ref, w_ref, mu0_ref, mup_ref, mun_ref, o_ref):
    tm = h_ref.shape[0]
    t0 = pl.program_id(0) * tm
    keep_prev = jnp.where(_is_sequence_edge(t0), 0.0, 1.0)
    keep_next = jnp.where(_is_sequence_edge(t0 + tm), 0.0, 1.0)
    lhs = jnp.concatenate([hp_ref[...], h_ref[...], hn_ref[...]], axis=0)
    ue = _dot(lhs, w_ref[...])
    n = tm + 2 * HALO
    down = pltpu.roll(ue, 1, axis=0)[HALO:HALO + tm]
    up = pltpu.roll(ue, n - 1, axis=0)[HALO:HALO + tm]
    row8 = lax.broadcasted_iota(jnp.int32, (8, 1), 0)
    head = jnp.where(row8 == 0, down[:8] * keep_prev, down[:8])
    tail = jnp.where(row8 == 7, up[tm - 8:] * keep_next, up[tm - 8:])
    prev = jnp.concatenate([head, down[8:]], axis=0)
    nxt = jnp.concatenate([up[:tm - 8], tail], axis=0)
    o_ref[...] = mu0_ref[...] * ue[HALO:HALO + tm] + mup_ref[...] * prev + mun_ref[...] * nxt


def _proj_shift(h, w, mu0, mup, mun, tn, name, tm=1024):
    n = w.shape[1]
    hb = tm // HALO
    nblk = T_ALL // HALO
    col = pl.BlockSpec((1, tn), lambda i, j: (0, j))
    return pl.pallas_call(
        _proj_shift_kernel,
        grid=(T_ALL // tm, n // tn),
        in_specs=[pl.BlockSpec((HALO, D_MODEL), lambda i, j: (jnp.maximum(i * hb - 1, 0), 0)),
                  pl.BlockSpec((tm, D_MODEL), lambda i, j: (i, 0)),
                  pl.BlockSpec((HALO, D_MODEL), lambda i, j: (jnp.minimum((i + 1) * hb, nblk - 1), 0)),
                  pl.BlockSpec((D_MODEL, tn), lambda i, j: (0, j)), col, col, col],
        out_specs=pl.BlockSpec((tm, tn), lambda i, j: (i, j)),
        out_shape=jax.ShapeDtypeStruct((T_ALL, n), f32),
        compiler_params=pltpu.CompilerParams(dimension_semantics=("parallel", "arbitrary"),
                                             vmem_limit_bytes=VMEM_LIMIT),
        name=name,
    )(h, h, h, w, mu0, mup, mun)


def _proj(h, w, tn, out_dtype, sigmoid, name, tm=1024):
    n = w.shape[1]
    return pl.pallas_call(
        functools.partial(_proj_kernel, sigmoid=sigmoid),
        grid=(T_ALL // tm, n // tn),
        in_specs=[pl.BlockSpec((tm, D_MODEL), lambda i, j: (i, 0)),
                  pl.BlockSpec((D_MODEL, tn), lambda i, j: (0, j))],
        out_specs=pl.BlockSpec((tm, tn), lambda i, j: (i, j)),
        out_shape=jax.ShapeDtypeStruct((T_ALL, n), out_dtype),
        compiler_params=pltpu.CompilerParams(dimension_semantics=("parallel", "arbitrary"),
                                             vmem_limit_bytes=VMEM_LIMIT),
        name=name,
    )(h, w)


def _attn_kernel(qn_ref, qr_ref, kn_ref, kr_ref, v_ref, o_ref, kcat_ref, vt_ref, s_ref, p_ref, *,
                 seq, tq, tk, unroll):
    head = pl.program_id(1)
    nsteps = seq // tk

    @pl.when(pl.program_id(2) == 0)
    def _():
        kcat_ref[:, :NOPE] = kn_ref[...]
        kcat_ref[:, NOPE:] = kr_ref[...]
        for c in range(seq // 512):
            cs = slice(c * 512, (c + 1) * 512)
            vt_ref[:, cs] = jnp.transpose(v_ref[cs, :].astype(f32)).astype(bf16)

    lane = lax.broadcasted_iota(jnp.int32, (tq, 128), 1)
    keep = (lane // ROPE) == (head % 2)
    qr = jnp.where(keep, qr_ref[...].astype(f32), 0.0).astype(bf16)
    q = jnp.concatenate([qn_ref[...], qr], axis=1)

    def scores(j, slot):
        rows = pl.ds(pl.multiple_of(j * tk, tk), tk)
        s_ref[slot] = _dot_nt(kcat_ref[rows, :], q)

    def absorb(j, slot, carry):
        m, l8, acc = carry
        s = s_ref[slot]
        m_loc = jnp.max(s.reshape(tk // 8, 8, tq), axis=0)
        m_new = jnp.maximum(m, jnp.max(m_loc, axis=0, keepdims=True))
        alpha = jnp.exp2(m - m_new)
        l8 = alpha * l8
        rc = 256
        for c in range(tk // rc):
            p = jnp.exp2(s_ref[slot, c * rc:(c + 1) * rc, :] - m_new)
            l8 = l8 + jnp.sum(p.reshape(rc // 8, 8, tq), axis=0)
            p_ref[c * rc:(c + 1) * rc, :] = p.astype(bf16)
        cols = pl.ds(pl.multiple_of(j * tk, tk), tk)
        acc = alpha * acc + _dot(vt_ref[:, cols], p_ref[...])
        return m_new, l8, acc

    def run(j0, carry, is_tail):
        for u in range(unroll):
            if not (is_tail and u == unroll - 1):
                scores(j0 + u + 1, (u + 1) % 2)
            carry = absorb(j0 + u, u % 2, carry)
        return carry

    carry = (jnp.full((1, tq), -jnp.inf, f32), jnp.zeros((8, tq), f32), jnp.zeros((V_DIM, tq), f32))
    scores(0, 0)
    carry = lax.fori_loop(0, nsteps // unroll - 1, lambda i, c: run(i * unroll, c, False), carry)
    _, l8, acc = run(nsteps - unroll, carry, True)
    out_t = acc / jnp.sum(l8, axis=0, keepdims=True)
    o_ref[...] = jnp.transpose(out_t).astype(o_ref.dtype)


def _attention(qn, qr, kn, kr, v, row_off, seq, nb, tq=256, tk=512, unroll=4):
    qoff = row_off // tq
    soff = row_off // seq
    nq = seq // tq
    return pl.pallas_call(
        functools.partial(_attn_kernel, seq=seq, tq=tq, tk=tk, unroll=unroll),
        grid=(nb, MLA_HEADS, nq),
        in_specs=[pl.BlockSpec((tq, NOPE), lambda b, h, i: (qoff + b * nq + i, h)),
                  pl.BlockSpec((tq, 128), lambda b, h, i: (qoff + b * nq + i, h // 2)),
                  pl.BlockSpec((seq, NOPE), lambda b, h, i: (soff + b, h)),
                  pl.BlockSpec((seq, 128), lambda b, h, i: (soff + b, 0)),
                  pl.BlockSpec((seq, V_DIM), lambda b, h, i: (soff + b, h))],
        out_specs=pl.BlockSpec((tq, V_DIM), lambda b, h, i: (b * nq + i, h)),
        out_shape=jax.ShapeDtypeStruct((nb * seq, D_MODEL), bf16),
        scratch_shapes=[pltpu.VMEM((seq, 2 * NOPE), bf16), pltpu.VMEM((V_DIM, seq), bf16),
                        pltpu.VMEM((2, tk, tq), f32), pltpu.VMEM((tk, tq), bf16)],
        compiler_params=pltpu.CompilerParams(
            dimension_semantics=("parallel", "parallel", "arbitrary"),
            vmem_limit_bytes=VMEM_LIMIT),
        name=f"mla_attention_s{seq}",
    )(qn, qr, kn, kr, v)


def _seg_sum(x, bd, split=True):
    cols = []
    for c in range(x.shape[1] // GROUP):
        xc = x[:, c * GROUP:(c + 1) * GROUP]
        cols.append(_split_dot(xc, bd) if split else _dot(xc.astype(bf16), bd))
    return jnp.concatenate(cols, axis=1)


def _store_groups(ref, x):
    for g in range(N_GROUPS):
        ref[g] = x[:, g * GROUP:(g + 1) * GROUP]


def _sigmoid(z):
    return 0.5 * jnp.tanh(0.5 * z) + 0.5


def _rwkv_prep_kernel(u_ref, h_ref, wg_ref,
                      w0f_ref, w2f_ref, w0b_ref, w2b_ref, a0f_ref, a2f_ref, a0b_ref, a2b_ref,
                      g2_ref, kkw_ref, kaw_ref, rk_ref, bd_ref,
                      r_ref, v_ref, kk_ref, kf_ref, kb_ref, bf_ref, bb_ref, lwf_ref, lwb_ref,
                      g_ref, bonus_ref, gates_ref):
    cw = 512
    h = h_ref[...]
    for c in range(2 * D_MODEL // cw):
        cs = slice(c * cw, (c + 1) * cw)
        gates_ref[:, cs] = _sigmoid(_dot(h, wg_ref[:, cs])).astype(bf16)

    d = D_MODEL
    r = u_ref[:, 0:d]
    k = u_ref[:, d:2 * d]
    v = u_ref[:, 2 * d:3 * d]
    xw_f = u_ref[:, 3 * d:3 * d + 128]
    xw_b = u_ref[:, 3 * d + 128:3 * d + 256]
    xa_f = u_ref[:, 3 * d + 256:3 * d + 384]
    xa_b = u_ref[:, 3 * d + 384:3 * d + 512]
    xg = u_ref[:, 3 * d + 512:U_PACKED]

    def neg_exp_logw(xw, w0_ref, w2_ref):
        z = w0_ref[...] + _dot(jnp.tanh(xw).astype(bf16), w2_ref[...])
        return (-np.exp(-0.5)) * _sigmoid(z)

    _store_groups(lwf_ref, neg_exp_logw(xw_f, w0f_ref, w2f_ref))
    _store_groups(lwb_ref, neg_exp_logw(xw_b, w0b_ref, w2b_ref))
    a_f = _sigmoid(a0f_ref[...] + _dot(xa_f.astype(bf16), a2f_ref[...]))
    a_b = _sigmoid(a0b_ref[...] + _dot(xa_b.astype(bf16), a2b_ref[...]))
    g_ref[...] = _dot(_sigmoid(xg).astype(bf16), g2_ref[...]).astype(bf16)

    bd = bd_ref[...]
    kk = k * kkw_ref[...]
    kk = kk * lax.rsqrt(jnp.maximum(_seg_sum(kk * kk, bd, split=False), 1e-24))
    ka = kaw_ref[...]
    k_f = k * (1.0 + (a_f - 1.0) * ka)
    k_b = k * (1.0 + (a_b - 1.0) * ka)
    bonus = _seg_sum(r * (k_f + k_b) * rk_ref[...], bd, split=False) * v

    _store_groups(r_ref, r.astype(bf16))
    _store_groups(v_ref, v.astype(bf16))
    _store_groups(kk_ref, kk.astype(bf16))
    _store_groups(kf_ref, k_f.astype(bf16))
    _store_groups(kb_ref, k_b.astype(bf16))
    _store_groups(bf_ref, (kk * a_f).astype(bf16))
    _store_groups(bb_ref, (kk * a_b).astype(bf16))
    bonus_ref[...] = bonus.astype(bf16)


def _rwkv_prep(u, h, w_g, vecs, loras, g2, bd, tm=128):
    vec = _const_spec((1, D_MODEL))
    lora = _const_spec((LORA_PAD, D_MODEL))
    w0f, w0b, a0f, a0b, kkw, kaw, rk = vecs
    w2f, w2b, a2f, a2b = loras
    out_bf = jax.ShapeDtypeStruct((T_ALL, D_MODEL), bf16)
    grp_bf = jax.ShapeDtypeStruct((N_GROUPS, T_ALL, GROUP), bf16)
    grp_f = jax.ShapeDtypeStruct((N_GROUPS, T_ALL, GROUP), f32)
    row = pl.BlockSpec((tm, D_MODEL), lambda i: (i, 0))
    grp = pl.BlockSpec((N_GROUPS, tm, GROUP), lambda i: (0, i, 0))
    return pl.pallas_call(
        _rwkv_prep_kernel,
        grid=(T_ALL // tm,),
        in_specs=[pl.BlockSpec((tm, U_PACKED), lambda i: (i, 0)), row, _const_spec(w_g.shape),
                  vec, lora, vec, lora, vec, lora, vec, lora,
                  _const_spec((GATE_LORA, D_MODEL)), vec, vec, vec, _const_spec((GROUP, GROUP))],
        out_specs=[grp] * 9 + [row, row, pl.BlockSpec((tm, 2 * D_MODEL), lambda i: (i, 0))],
        out_shape=[grp_bf] * 7 + [grp_f, grp_f, out_bf, out_bf,
                                  jax.ShapeDtypeStruct((T_ALL, 2 * D_MODEL), bf16)],
        compiler_params=pltpu.CompilerParams(dimension_semantics=("parallel",),
                                             vmem_limit_bytes=VMEM_LIMIT),
        name="rwkv_prep_gates",
    )(u, h, w_g, w0f, w2f, w0b, w2b, a0f, a2f, a0b, a2b, g2, kkw, kaw, rk, bd)


def _wkv_masks(reverse):
    C = CHUNK
    t_i = lax.broadcasted_iota(jnp.int32, (C, GROUP), 0)
    s_i = lax.broadcasted_iota(jnp.int32, (C, GROUP), 1) % C
    strict = (s_i > t_i) if reverse else (s_i < t_i)
    incl = (s_i >= t_i) if reverse else (s_i <= t_i)
    eye = jnp.where(s_i == t_i, 1.0, 0.0)
    tt = lax.broadcasted_iota(jnp.int32, (C, C), 0)
    ss = lax.broadcasted_iota(jnp.int32, (C, C), 1)
    tri = jnp.where((ss >= tt) if reverse else (ss <= tt), 1.0, 0.0).astype(bf16)
    return strict, incl, eye, tri


def _wkv_chunk_operators(chains, bdmask):
    C = CHUNK
    n = range(len(chains))
    rs, ks, vs, kks, bs, lws, masks, revs = zip(*chains)
    strict = [m[0] for m in masks]
    incl = [m[1] for m in masks]
    eye = [m[2] for m in masks]
    tri = [m[3] for m in masks]
    last = [0 if rev else C - 1 for rev in revs]
    even_blk = (lax.broadcasted_iota(jnp.int32, (2 * C, GROUP), 1) // RWKV_HEAD) % 2 == 0

    def bd(x):
        return jnp.where(bdmask, jnp.concatenate([x] * 4, axis=0), 0.0).astype(bf16)

    def ss_mul(x, y_bd):
        return _dot(x.astype(bf16), y_bd)

    lw_hi = [lws[i].astype(bf16) for i in n]
    lw_lo = [(lws[i] - lw_hi[i].astype(f32)).astype(bf16) for i in n]
    cum = [_dot(tri[i], lw_hi[i]) + _dot(tri[i], lw_lo[i]) for i in n]
    w_inv = [jnp.exp(-cum[i]) for i in n]
    w_rem = [jnp.exp(cum[i][last[i]:last[i] + 1, :] - cum[i]) for i in n]
    a_t = [-kks[i] * jnp.exp(cum[i] - lws[i]) for i in n]
    r_t = [rs[i] * jnp.exp(cum[i]) for i in n]

    lhs = [jnp.concatenate([a_t[i], r_t[i]], axis=0).astype(bf16) for i in n]
    b_i = [bs[i] * w_inv[i] for i in n]
    k_i = [ks[i] * w_inv[i] for i in n]
    bk_t = [jnp.transpose(jnp.concatenate([b_i[i], k_i[i]], axis=0)) for i in n]
    kb_t = [jnp.transpose(jnp.concatenate([k_i[i], b_i[i]], axis=0)) for i in n]
    mix1 = [_dot(lhs[i], jnp.where(bdmask, jnp.concatenate([bk_t[i]] * 2, axis=1), 0.0).astype(bf16))
            for i in n]
    mix2 = [_dot(lhs[i], jnp.where(bdmask, jnp.concatenate([kb_t[i]] * 2, axis=1), 0.0).astype(bf16))
            for i in n]
    ab = [jnp.where(even_blk, mix1[i], mix2[i]) for i in n]
    ak = [jnp.where(even_blk, mix2[i], mix1[i]) for i in n]
    a_ab = [jnp.where(strict[i], ab[i][:C], 0.0) for i in n]
    a_rb = [jnp.where(incl[i], ab[i][C:], 0.0) for i in n]
    a_k = [jnp.concatenate([jnp.where(strict[i], ak[i][:C], 0.0),
                            jnp.where(incl[i], ak[i][C:], 0.0)], axis=0) for i in n]

    tinv = [eye[i] + a_ab[i] for i in n]
    pw = [ss_mul(a_ab[i], bd(a_ab[i])) for i in n]
    for _ in range(4):
        both = [ss_mul(jnp.concatenate([tinv[i], pw[i]], axis=0), bd(pw[i])) for i in n]
        tinv = [tinv[i] + both[i][:C] for i in n]
        pw = [both[i][C:] for i in n]
    tinv = [tinv[i] + ss_mul(tinv[i], bd(pw[i])) for i in n]

    zr = [ss_mul(a_k[i], bd(vs[i])) for i in n]
    pq = [ss_mul(tinv[i], jnp.concatenate([bd(a_t[i]), bd(zr[i][:C])], axis=1)) for i in n]
    p = [pq[i][:, :GROUP] for i in n]
    q = [pq[i][:, GROUP:] for i in n]
    yy = [ss_mul(a_rb[i], jnp.concatenate([bd(p[i]), bd(q[i])], axis=1)) for i in n]
    y_op = [r_t[i] + yy[i][:, :GROUP] for i in n]
    y_c = [yy[i][:, GROUP:] + zr[i][C:] for i in n]
    py = [jnp.concatenate([p[i], y_op[i]], axis=0).astype(bf16) for i in n]
    bkt = [jnp.transpose(jnp.concatenate([bs[i] * w_rem[i], ks[i] * w_rem[i]], axis=0)).astype(bf16)
           for i in n]
    w_tot = [jnp.broadcast_to(jnp.exp(jnp.transpose(cum[i])[:, last[i]:last[i] + 1]), (GROUP, 128))
             for i in n]
    return [(py[i], q[i], y_c[i], bkt[i], w_tot[i]) for i in n]


def _wkv_kernel(rf_ref, vf_ref, kkf_ref, kf_ref, bf_ref, lwf_ref,
                rb_ref, vb_ref, kkb_ref, kb_ref, bb_ref, lwb_ref,
                yf_ref, yb_ref,
                h_ref, py_ref, q_ref, yc_ref, bkt_ref, wt_ref, *, cb, ng):
    C = CHUNK

    c = pl.program_id(1)
    for d, edge in enumerate((c * (cb * C), (pl.num_programs(1) - c) * (cb * C))):
        @pl.when(_is_sequence_edge(edge))
        def _():
            for g in range(ng):
                h_ref[g, d] = jnp.zeros((GROUP, GROUP), f32)

    br = lax.broadcasted_iota(jnp.int32, (GROUP, GROUP), 0) // RWKV_HEAD
    bc = lax.broadcasted_iota(jnp.int32, (GROUP, GROUP), 1) // RWKV_HEAD
    bdmask = br == bc
    dirs = ((rf_ref, vf_ref, kkf_ref, kf_ref, bf_ref, lwf_ref, yf_ref, _wkv_masks(False)),
            (rb_ref, vb_ref, kkb_ref, kb_ref, bb_ref, lwb_ref, yb_ref, _wkv_masks(True)))

    def operators(gp, carry):
        chains, slots = [], []
        for g in (2 * gp, 2 * gp + 1):
            for j in range(cb):
                rows = pl.ds(j * C, C)
                for d, (r_ref, v_ref, kk_ref, k_ref, b_ref, lw_ref, _, masks) in enumerate(dirs):
                    chains.append((r_ref[g, rows, :].astype(f32), k_ref[g, rows, :].astype(f32),
                                   v_ref[g, rows, :].astype(f32), kk_ref[g, rows, :].astype(f32),
                                   b_ref[g, rows, :].astype(f32), lw_ref[g, rows, :], masks, d == 1))
                    slots.append((g, d, j))
        for (g, d, j), (py, q, y_c, bkt, wt) in zip(slots, _wkv_chunk_operators(chains, bdmask)):
            py_ref[g, d, j] = py
            q_ref[g, d, j] = q
            yc_ref[g, d, j] = y_c
            bkt_ref[g, d, j] = bkt
            wt_ref[g, d, j] = wt
        return carry

    lax.fori_loop(0, ng // 2, operators, 0)

    def recur(j, carry):
        jd = (j, cb - 1 - j)
        rows = [pl.ds(pl.multiple_of(jd[d] * C, C), C) for d in range(2)]
        gd = [(g, d) for g in range(ng) for d in range(2)]
        h = [h_ref[g, d] for g, d in gd]
        uy = [_dot(py_ref[g, d, jd[d]], h[i].astype(bf16)) for i, (g, d) in enumerate(gd)]
        uv = []
        for i, (g, d) in enumerate(gd):
            dirs[d][6][g, rows[d], :] = uy[i][C:] + yc_ref[g, d, jd[d]]
            u = uy[i][:C] + q_ref[g, d, jd[d]]
            uv.append(jnp.concatenate([u.astype(bf16), dirs[d][1][g, rows[d], :]], axis=0))
        upd = [_dot(bkt_ref[g, d, jd[d]], uv[i]) for i, (g, d) in enumerate(gd)]
        for i, (g, d) in enumerate(gd):
            wt = wt_ref[g, d, jd[d]]
            h_ref[g, d] = h[i] * jnp.concatenate([wt, wt], axis=1) + jnp.where(bdmask, upd[i], 0.0)
        return carry

    lax.fori_loop(0, cb, recur, 0)


def _wkv_scan(r, v, kk, k_f, b_f, lw_f, k_b, b_b, lw_b, cb=4, ng=4):
    C = CHUNK
    rb = cb * C
    nblk = T_ALL // rb
    fwd = pl.BlockSpec((ng, rb, GROUP), lambda g, c: (g, c, 0))
    bwd = pl.BlockSpec((ng, rb, GROUP), lambda g, c: (g, nblk - 1 - c, 0))
    out = jax.ShapeDtypeStruct((N_GROUPS, T_ALL, GROUP), f32)
    return pl.pallas_call(
        functools.partial(_wkv_kernel, cb=cb, ng=ng),
        grid=(N_GROUPS // ng, nblk),
        in_specs=[fwd] * 6 + [bwd] * 6,
        out_specs=[fwd, bwd],
        out_shape=[out, out],
        scratch_shapes=[pltpu.VMEM((ng, 2, GROUP, GROUP), f32),
                        pltpu.VMEM((ng, 2, cb, 2 * C, GROUP), bf16),
                        pltpu.VMEM((ng, 2, cb, C, GROUP), f32),
                        pltpu.VMEM((ng, 2, cb, C, GROUP), f32),
                        pltpu.VMEM((ng, 2, cb, GROUP, 2 * C), bf16),
                        pltpu.VMEM((ng, 2, cb, GROUP, 128), f32)],
        compiler_params=pltpu.CompilerParams(
            dimension_semantics=("parallel", "arbitrary"),
            vmem_limit_bytes=VMEM_LIMIT),
        name="wkv_scan",
    )(r, v, kk, k_f, b_f, lw_f, r, v, kk, k_b, b_b, lw_b)


def _branch_kernel(yap_ref, yas_ref, yf_ref, yb_ref, bonus_ref, g_ref, g1_ref, g2_ref, lnw_ref, lnb_ref,
                   bd_ref, wa_ref, wb_ref, o_ref):
    bd = bd_ref[...]
    y = jnp.concatenate([yf_ref[g] + yb_ref[g] for g in range(N_GROUPS)], axis=1)
    inv_n = 1.0 / RWKV_HEAD
    mean = _seg_sum(y, bd) * inv_n
    dlt = y - mean
    var = _seg_sum(dlt * dlt, bd) * inv_n
    yn = dlt * lax.rsqrt(var + GN_EPS) * lnw_ref[...] + lnb_ref[...]
    yb = ((yn + bonus_ref[...].astype(f32)) * g_ref[...].astype(f32)).astype(bf16)
    ya = _pick_x(yap_ref, yas_ref)
    cw = 512
    for c in range(D_MODEL // cw):
        cs = slice(c * cw, (c + 1) * cw)
        oa = _dot(ya, wa_ref[:, cs])
        ob = _dot(yb, wb_ref[:, cs])
        o_ref[:, cs] = (g1_ref[:, cs].astype(f32) * oa + g2_ref[:, cs].astype(f32) * ob).astype(bf16)


def _branch(ya_p, ya_s, yf, yb, bonus, g, gates, lnw, lnb, bd, wa, wb, tm=256):
    row = pl.BlockSpec((tm, D_MODEL), lambda i: (i, 0))
    grp = pl.BlockSpec((N_GROUPS, tm, GROUP), lambda i: (0, i, 0))
    return pl.pallas_call(
        _branch_kernel,
        grid=(T_ALL // tm,),
        in_specs=_x_specs(tm) + [grp, grp, row, row,
                  pl.BlockSpec((tm, D_MODEL), lambda i: (i, 0)),
                  pl.BlockSpec((tm, D_MODEL), lambda i: (i, 1)),
                  _const_spec((1, D_MODEL)), _const_spec((1, D_MODEL)), _const_spec((GROUP, GROUP)),
                  _const_spec((D_MODEL, D_MODEL)), _const_spec((D_MODEL, D_MODEL))],
        out_specs=row,
        out_shape=jax.ShapeDtypeStruct((T_ALL, D_MODEL), bf16),
        compiler_params=pltpu.CompilerParams(dimension_semantics=("parallel",),
                                             vmem_limit_bytes=VMEM_LIMIT),
        name="branch_mix",
    )(ya_p, ya_s, yf, yb, bonus, g, gates, gates, lnw, lnb, bd, wa, wb)


def _out_proj_kernel(o_ref, xp_ref, xs_ref, w_ref, g_ref, x1_ref):
    mix = _dot(o_ref[...], w_ref[...])
    x1_ref[...] = _pick_x(xp_ref, xs_ref) + _rms(mix, g_ref[...])


def _out_proj(o, xp, xs, w, g, tm=512):
    row = pl.BlockSpec((tm, D_MODEL), lambda i: (i, 0))
    return pl.pallas_call(
        _out_proj_kernel,
        grid=(T_ALL // tm,),
        in_specs=[row] + _x_specs(tm) + [_const_spec((D_MODEL, D_MODEL)), _const_spec((1, D_MODEL))],
        out_specs=row,
        out_shape=jax.ShapeDtypeStruct((T_ALL, D_MODEL), f32),
        compiler_params=pltpu.CompilerParams(dimension_semantics=("parallel",),
                                             vmem_limit_bytes=VMEM_LIMIT),
        name="out_proj",
    )(o, xp, xs, w, g)


def _mlp_kernel(x_ref, gpre_ref, wu_ref, wd_ref, gpost_ref, op_ref, os_ref, xn_ref, acc_ref):
    i = pl.program_id(0)
    j = pl.program_id(1)
    n_prompt_tiles = T_PROMPT // x_ref.shape[0]

    @pl.when(j == 0)
    def _():
        xn_ref[...] = _rms(x_ref[...], gpre_ref[...]).astype(bf16)
        acc_ref[...] = jnp.zeros_like(acc_ref)

    h = jnp.maximum(_dot(xn_ref[...], wu_ref[...]), 0.0)
    acc_ref[...] += _dot((h * h).astype(bf16), wd_ref[...])

    last = j == pl.num_programs(1) - 1

    @pl.when(jnp.logical_and(last, i < n_prompt_tiles))
    def _():
        op_ref[...] = x_ref[...] + _rms(acc_ref[...], gpost_ref[...])

    @pl.when(jnp.logical_and(last, i >= n_prompt_tiles))
    def _():
        os_ref[...] = x_ref[...] + _rms(acc_ref[...], gpost_ref[...])


def _mlp(x1, gpre, w_up, w_down, gpost, tm=512, tf=1024):
    npb = T_PROMPT // tm
    row = pl.BlockSpec((tm, D_MODEL), lambda i, j: (i, 0))
    out = jax.ShapeDtypeStruct((T_PROMPT, D_MODEL), f32)
    return pl.pallas_call(
        _mlp_kernel,
        grid=(T_ALL // tm, D_FF // tf),
        in_specs=[row, _const_spec((1, D_MODEL)),
                  pl.BlockSpec((D_MODEL, tf), lambda i, j: (0, j)),
                  pl.BlockSpec((tf, D_MODEL), lambda i, j: (j, 0)),
                  _const_spec((1, D_MODEL))],
        out_specs=[pl.BlockSpec((tm, D_MODEL), lambda i, j: (jnp.minimum(i, npb - 1), 0)),
                   pl.BlockSpec((tm, D_MODEL), lambda i, j: (jnp.maximum(i - npb, 0), 0))],
        out_shape=[out, out],
        scratch_shapes=[pltpu.VMEM((tm, D_MODEL), bf16), pltpu.VMEM((tm, D_MODEL), f32)],
        compiler_params=pltpu.CompilerParams(dimension_semantics=("arbitrary", "arbitrary"),
                                             vmem_limit_bytes=VMEM_LIMIT),
        name="mlp",
    )(x1, gpre, w_up, w_down, gpost)


def _pad_lanes(a, width):
    return jnp.pad(a, [(0, 0)] * (a.ndim - 1) + [(0, width - a.shape[-1])])


def _pack_u(a):
    d = D_MODEL
    cuts = [3 * d, 3 * d + 96, 3 * d + 192, 3 * d + 288, 3 * d + 384]
    return jnp.concatenate(
        [a[..., :cuts[0]]]
        + [_pad_lanes(a[..., cuts[i]:cuts[i + 1]], LORA_PAD) for i in range(4)]
        + [a[..., cuts[4]:]], axis=-1)


def _pad_rows(w):
    return jnp.pad(w, [(0, LORA_PAD - w.shape[0]), (0, 0)])


def _rope_tables():
    inv = 1.0 / (ROPE_THETA ** (jnp.arange(0, ROPE, 2, dtype=f32) / ROPE))
    pos = jnp.concatenate([jnp.arange(T_PROMPT, dtype=f32),
                           jnp.tile(jnp.arange(S_SAMPLE, dtype=f32), B_SAMPLE)])
    ang = pos[:, None] * inv[None, :]
    cos, sin = jnp.cos(ang), jnp.sin(ang)
    cos = jnp.concatenate([cos, cos] * 2, axis=1)
    sin = jnp.concatenate([-sin, sin] * 2, axis=1)
    return cos, sin


def kernel(x_prompt, x_sample, norm_pre_mix, w_in, mu_prev, mu_next, mla_q_norm, mla_w_uq, mla_kv_norm, mla_w_ukv, rwkv_w0_f, rwkv_w2_f, rwkv_w0_b, rwkv_w2_b, rwkv_a0_f, rwkv_a2_f, rwkv_a0_b, rwkv_a2_b, rwkv_g2, rwkv_k_k, rwkv_k_a, rwkv_r_k, rwkv_ln_w, rwkv_ln_b, w_branch, w_out, norm_post_mix, norm_pre_mlp, w_mlp_up, w_mlp_down, norm_post_mlp):
    xp = x_prompt.reshape(T_PROMPT, D_MODEL)
    xs = x_sample.reshape(B_SAMPLE * S_SAMPLE, D_MODEL)

    swap = np.concatenate([np.arange(ROPE // 2, ROPE), np.arange(ROPE // 2)])
    w = w_in[0].astype(bf16)
    mla_in = Q_LORA + KV_LORA + ROPE
    rwkv_in = 3 * D_MODEL + 4 * 96 + GATE_LORA
    w_kr = w[:, Q_LORA + KV_LORA:mla_in]
    w_krs = w_kr[:, swap]
    w_mla = jnp.concatenate([w[:, :Q_LORA + KV_LORA], w_kr, w_kr, w_krs, w_krs], axis=1)
    w_u = _pack_u(w[:, mla_in:mla_in + rwkv_in])
    w_g = w[:, mla_in + rwkv_in:]
    mup = _pack_u(mu_prev)
    mun = _pack_u(mu_next)

    wq3 = mla_w_uq[0].reshape(Q_LORA, MLA_HEADS, QK_DIM)
    wq_rope = wq3[:, :, NOPE:]
    w_q = jnp.concatenate([wq3[:, :, :NOPE].reshape(Q_LORA, -1),
                           wq_rope.reshape(Q_LORA, -1),
                           wq_rope[:, :, swap].reshape(Q_LORA, -1)], axis=1).astype(bf16)
    wkv3 = mla_w_ukv[0].reshape(KV_LORA, MLA_HEADS, NOPE + V_DIM)
    w_kv = jnp.concatenate([wkv3[:, :, :NOPE].reshape(KV_LORA, -1),
                            wkv3[:, :, NOPE:].reshape(KV_LORA, -1)], axis=1).astype(bf16)
    cos, sin = _rope_tables()

    qn, qr, kn, v, kr, h = _mla_front(xp, xs, norm_pre_mix, w_mla, mla_q_norm, w_q, mla_kv_norm, w_kv,
                                      cos, sin)
    ya_p = _attention(qn, qr, kn, kr, v, 0, T_PROMPT, 1, tq=512, tk=1024, unroll=4)
    ya_s = _attention(qn, qr, kn, kr, v, T_PROMPT, S_SAMPLE, B_SAMPLE, tq=512, tk=1024, unroll=2)

    u = _proj_shift(h, w_u, 1.0 - mup - mun, mup, mun, 768, "proj_rwkv")
    head_blk = np.arange(GROUP) // RWKV_HEAD
    bd = jnp.asarray(head_blk[:, None] == head_blk[None, :], dtype=bf16)
    vecs = (rwkv_w0_f, rwkv_w0_b, rwkv_a0_f, rwkv_a0_b, rwkv_k_k, rwkv_k_a,
            rwkv_r_k.reshape(1, D_MODEL))
    loras = tuple(_pad_rows(a[0]).astype(bf16) for a in (rwkv_w2_f, rwkv_w2_b, rwkv_a2_f, rwkv_a2_b))
    (r, vv, kk, k_f, k_b, b_f, b_b, lw_f, lw_b, g, bonus, gates) = _rwkv_prep(
        u, h, w_g, vecs, loras, rwkv_g2[0].astype(bf16), bd)
    y_f, y_b = _wkv_scan(r, vv, kk, k_f, b_f, lw_f, k_b, b_b, lw_b)

    wb = w_branch[0].astype(bf16)
    o = _branch(ya_p, ya_s, y_f, y_b, bonus, g, gates, rwkv_ln_w, rwkv_ln_b, bd,
                wb[:D_MODEL], wb[D_MODEL:])
    x1 = _out_proj(o, xp, xs, w_out[0].astype(bf16), norm_post_mix)
    out_p, out_s = _mlp(x1, norm_pre_mlp, w_mlp_up[0].astype(bf16), w_mlp_down[0].astype(bf16),
                        norm_post_mlp)
    return (out_p.reshape(1, T_PROMPT, D_MODEL), out_s.reshape(B_SAMPLE, S_SAMPLE, D_MODEL))
```

```python
import functools

import numpy as np
import jax
import jax.numpy as jnp
from jax import lax
from jax.experimental import pallas as pl
from jax.experimental.pallas import tpu as pltpu

f32 = jnp.float32
bf16 = jnp.bfloat16

D_MODEL = 2048
T_PROMPT = 8192
S_SAMPLE = 2048
B_SAMPLE = 4
T_ALL = T_PROMPT + B_SAMPLE * S_SAMPLE

MLA_HEADS = 16
Q_LORA = 768
KV_LORA = 512
NOPE = 128
ROPE = 64
V_DIM = 128
QK_DIM = NOPE + ROPE
ROPE_THETA = 10000.0
LOG2_E = 1.4426950408889634

RWKV_HEAD = 64
RWKV_HEADS = 32
LORA_PAD = 128
GATE_LORA = 256
U_PACKED = 3 * D_MODEL + 4 * LORA_PAD + GATE_LORA

D_FF = 4 * D_MODEL
NORM_EPS = 1e-6
GN_EPS = 64e-5

CHUNK = 64
GROUP = 256
N_GROUPS = D_MODEL // GROUP

VMEM_LIMIT = 56 * 1024 * 1024
TM_MLA_FRONT = 512
TM_PROJ, TN_PROJ = 1024, 768
TQ_ATTN, TK_ATTN = 1024, 1024
TM_PREP = 128
WKV_CHUNKS, WKV_GROUPS = 4, 4
TM_BRANCH = 256
TM_OUT_PROJ = 512
TM_MLP, TF_MLP = 512, 1024


def _rms(x, g):
    return x * lax.rsqrt(jnp.mean(x * x, axis=-1, keepdims=True) + NORM_EPS) * g


def _dot(a, b):
    return jnp.dot(a, b, preferred_element_type=f32)


def _dot_nt(a, b):
    return lax.dot_general(a, b, (((1,), (1,)), ((), ())), preferred_element_type=f32)


def _split_dot(x, w):
    hi = x.astype(bf16)
    lo = (x - hi.astype(f32)).astype(bf16)
    return _dot(hi, w) + _dot(lo, w)


def _const_spec(shape):
    return pl.BlockSpec(shape, lambda *_: (0,) * len(shape), pipeline_mode=pl.Buffered(1))


def _x_specs(tm):
    npb = T_PROMPT // tm
    return [pl.BlockSpec((tm, D_MODEL), lambda i, *_: (jnp.minimum(i, npb - 1), 0)),
            pl.BlockSpec((tm, D_MODEL), lambda i, *_: (jnp.maximum(i - npb, 0), 0))]


def _is_sequence_edge(row):
    return jnp.logical_or(row == 0, jnp.logical_and(row >= T_PROMPT, (row - T_PROMPT) % S_SAMPLE == 0))


def _pick_x(xp_ref, xs_ref):
    tm = xp_ref.shape[0]
    return jnp.where(pl.program_id(0) < T_PROMPT // tm, xp_ref[...], xs_ref[...])


def _mla_front_kernel(xp_ref, xs_ref, g_ref, wm_ref, qg_ref, wq_ref, kvg_ref, wkv_ref, cos_ref, sin_ref,
                      qn_ref, qr_ref, kn_ref, v_ref, kr_ref, h_ref):
    h = _rms(_pick_x(xp_ref, xs_ref), g_ref[...]).astype(bf16)
    h_ref[...] = h
    pm = _dot(h, wm_ref[...])
    cos = cos_ref[...]
    sin = sin_ref[...]
    kr_ref[...] = (pm[:, 1280:1408] * cos + pm[:, 1408:1536] * sin).astype(bf16)
    qin = _rms(pm[:, :Q_LORA], qg_ref[...]).astype(bf16)
    kvin = _rms(pm[:, Q_LORA:Q_LORA + KV_LORA], kvg_ref[...]).astype(bf16)
    scale = QK_DIM ** -0.5 * LOG2_E
    cw = 512
    for c in range(D_MODEL // cw):
        q = _dot(qin, wq_ref[:, c * cw:(c + 1) * cw])
        qn_ref[:, c * cw:(c + 1) * cw] = (q * scale).astype(bf16)
    cos4 = jnp.concatenate([cos] * (cw // 128), axis=1)
    sin4 = jnp.concatenate([sin] * (cw // 128), axis=1)
    nrope = MLA_HEADS * ROPE
    for c in range(nrope // cw):
        qa = _dot(qin, wq_ref[:, D_MODEL + c * cw:D_MODEL + (c + 1) * cw])
        qb = _dot(qin, wq_ref[:, D_MODEL + nrope + c * cw:D_MODEL + nrope + (c + 1) * cw])
        qr_ref[:, c * cw:(c + 1) * cw] = ((qa * cos4 + qb * sin4) * scale).astype(bf16)
    for c in range(D_MODEL // cw):
        kn_ref[:, c * cw:(c + 1) * cw] = _dot(kvin, wkv_ref[:, c * cw:(c + 1) * cw]).astype(bf16)
        v_ref[:, c * cw:(c + 1) * cw] = _dot(
            kvin, wkv_ref[:, D_MODEL + c * cw:D_MODEL + (c + 1) * cw]).astype(bf16)


def _mla_front(xp, xs, g, w_mla, qg, w_q, kvg, w_kv, cos, sin, tm=TM_MLA_FRONT):
    nq = w_q.shape[1]
    row = lambda w: pl.BlockSpec((tm, w), lambda i: (i, 0))
    return pl.pallas_call(
        _mla_front_kernel,
        grid=(T_ALL // tm,),
        in_specs=_x_specs(tm) + [_const_spec((1, D_MODEL)), _const_spec(w_mla.shape),
                  _const_spec((1, Q_LORA)), _const_spec((Q_LORA, nq)),
                  _const_spec((1, KV_LORA)), _const_spec(w_kv.shape), row(128), row(128)],
        out_specs=[row(D_MODEL), row(MLA_HEADS * ROPE), row(D_MODEL), row(D_MODEL), row(128),
                   row(D_MODEL)],
        out_shape=[jax.ShapeDtypeStruct((T_ALL, D_MODEL), bf16),
                   jax.ShapeDtypeStruct((T_ALL, MLA_HEADS * ROPE), bf16),
                   jax.ShapeDtypeStruct((T_ALL, D_MODEL), bf16),
                   jax.ShapeDtypeStruct((T_ALL, D_MODEL), bf16),
                   jax.ShapeDtypeStruct((T_ALL, 128), bf16),
                   jax.ShapeDtypeStruct((T_ALL, D_MODEL), bf16)],
        compiler_params=pltpu.CompilerParams(dimension_semantics=("parallel",),
                                             vmem_limit_bytes=VMEM_LIMIT),
        name="mla_front",
    )(xp, xs, g, w_mla, qg, w_q, kvg, w_kv, cos, sin)


HALO = 16


def _proj_shift_kernel(hp_ref, h_ref, hn_ref, w_ref, mu0_ref, mup_ref, mun_ref, o_ref):
    tm = h_ref.shape[0]
    t0 = pl.program_id(0) * tm
    keep_prev = jnp.where(_is_sequence_edge(t0), 0.0, 1.0)
    keep_next = jnp.where(_is_sequence_edge(t0 + tm), 0.0, 1.0)
    lhs = jnp.concatenate([hp_ref[...], h_ref[...], hn_ref[...]], axis=0)
    ue = _dot(lhs, w_ref[...])
    n = tm + 2 * HALO
    down = pltpu.roll(ue, 1, axis=0)[HALO:HALO + tm]
    up = pltpu.roll(ue, n - 1, axis=0)[HALO:HALO + tm]
    row8 = lax.broadcasted_iota(jnp.int32, (8, 1), 0)
    head = jnp.where(row8 == 0, down[:8] * keep_prev, down[:8])
    tail = jnp.where(row8 == 7, up[tm - 8:] * keep_next, up[tm - 8:])
    prev = jnp.concatenate([head, down[8:]], axis=0)
    nxt = jnp.concatenate([up[:tm - 8], tail], axis=0)
    o_ref[...] = mu0_ref[...] * ue[HALO:HALO + tm] + mup_ref[...] * prev + mun_ref[...] * nxt


def _proj_shift(h, w, mu0, mup, mun, name, tm=TM_PROJ, tn=TN_PROJ):
    n = w.shape[1]
    hb = tm // HALO
    nblk = T_ALL // HALO
    col = pl.BlockSpec((1, tn), lambda i, j: (0, j))
    return pl.pallas_call(
        _proj_shift_kernel,
        grid=(T_ALL // tm, n // tn),
        in_specs=[pl.BlockSpec((HALO, D_MODEL), lambda i, j: (jnp.maximum(i * hb - 1, 0), 0)),
                  pl.BlockSpec((tm, D_MODEL), lambda i, j: (i, 0)),
                  pl.BlockSpec((HALO, D_MODEL), lambda i, j: (jnp.minimum((i + 1) * hb, nblk - 1), 0)),
                  pl.BlockSpec((D_MODEL, tn), lambda i, j: (0, j)), col, col, col],
        out_specs=pl.BlockSpec((tm, tn), lambda i, j: (i, j)),
        out_shape=jax.ShapeDtypeStruct((T_ALL, n), f32),
        compiler_params=pltpu.CompilerParams(dimension_semantics=("parallel", "arbitrary"),
                                             vmem_limit_bytes=VMEM_LIMIT),
        name=name,
    )(h, h, h, w, mu0, mup, mun)


def _attn_kernel(qn_ref, qr_ref, kn_ref, kr_ref, v_ref, o_ref, kcat_ref, vt_ref, s_ref, p_ref, *,
                 seq, tq, tk, unroll):
    head = pl.program_id(1)
    nsteps = seq // tk

    @pl.when(pl.program_id(2) == 0)
    def _():
        kcat_ref[:, :NOPE] = kn_ref[...]
        kcat_ref[:, NOPE:] = kr_ref[...]
        for c in range(seq // 512):
            cs = slice(c * 512, (c + 1) * 512)
            vt_ref[:, cs] = jnp.transpose(v_ref[cs, :].astype(f32)).astype(bf16)

    lane = lax.broadcasted_iota(jnp.int32, (tq, 128), 1)
    keep = (lane // ROPE) == (head % 2)
    qr = jnp.where(keep, qr_ref[...].astype(f32), 0.0).astype(bf16)
    q = jnp.concatenate([qn_ref[...], qr], axis=1)

    def scores(j, slot):
        rows = pl.ds(pl.multiple_of(j * tk, tk), tk)
        s_ref[slot] = _dot_nt(kcat_ref[rows, :], q)

    def absorb(j, slot, carry):
        m, l8, acc = carry
        s = s_ref[slot]
        m_loc = jnp.max(s.reshape(tk // 8, 8, tq), axis=0)
        m_new = jnp.maximum(m, jnp.max(m_loc, axis=0, keepdims=True))
        alpha = jnp.exp2(m - m_new)
        l8 = alpha * l8
        rc = 256
        for c in range(tk // rc):
            p = jnp.exp2(s_ref[slot, c * rc:(c + 1) * rc, :] - m_new)
            l8 = l8 + jnp.sum(p.reshape(rc // 8, 8, tq), axis=0)
            p_ref[c * rc:(c + 1) * rc, :] = p.astype(bf16)
        cols = pl.ds(pl.multiple_of(j * tk, tk), tk)
        acc = alpha * acc + _dot(vt_ref[:, cols], p_ref[...])
        return m_new, l8, acc

    def run(j0, carry, is_tail):
        for u in range(unroll):
            if not (is_tail and u == unroll - 1):
                scores(j0 + u + 1, (u + 1) % 2)
            carry = absorb(j0 + u, u % 2, carry)
        return carry

    carry = (jnp.full((1, tq), -jnp.inf, f32), jnp.zeros((8, tq), f32), jnp.zeros((V_DIM, tq), f32))
    scores(0, 0)
    carry = lax.fori_loop(0, nsteps // unroll - 1, lambda i, c: run(i * unroll, c, False), carry)
    _, l8, acc = run(nsteps - unroll, carry, True)
    out_t = acc / jnp.sum(l8, axis=0, keepdims=True)
    o_ref[...] = jnp.transpose(out_t).astype(o_ref.dtype)


def _attention(qn, qr, kn, kr, v, row_off, seq, nb, unroll, tq=TQ_ATTN, tk=TK_ATTN):
    qoff = row_off // tq
    soff = row_off // seq
    nq = seq // tq
    return pl.pallas_call(
        functools.partial(_attn_kernel, seq=seq, tq=tq, tk=tk, unroll=unroll),
        grid=(nb, MLA_HEADS, nq),
        in_specs=[pl.BlockSpec((tq, NOPE), lambda b, h, i: (qoff + b * nq + i, h)),
                  pl.BlockSpec((tq, 128), lambda b, h, i: (qoff + b * nq + i, h // 2)),
                  pl.BlockSpec((seq, NOPE), lambda b, h, i: (soff + b, h)),
                  pl.BlockSpec((seq, 128), lambda b, h, i: (soff + b, 0)),
                  pl.BlockSpec((seq, V_DIM), lambda b, h, i: (soff + b, h))],
        out_specs=pl.BlockSpec((tq, V_DIM), lambda b, h, i: (b * nq + i, h)),
        out_shape=jax.ShapeDtypeStruct((nb * seq, D_MODEL), bf16),
        scratch_shapes=[pltpu.VMEM((seq, 2 * NOPE), bf16), pltpu.VMEM((V_DIM, seq), bf16),
                        pltpu.VMEM((2, tk, tq), f32), pltpu.VMEM((tk, tq), bf16)],
        compiler_params=pltpu.CompilerParams(
            dimension_semantics=("parallel", "parallel", "arbitrary"),
            vmem_limit_bytes=VMEM_LIMIT),
        name=f"mla_attention_s{seq}",
    )(qn, qr, kn, kr, v)


def _seg_sum(x, bd, split=True):
    cols = []
    for c in range(x.shape[1] // GROUP):
        xc = x[:, c * GROUP:(c + 1) * GROUP]
        cols.append(_split_dot(xc, bd) if split else _dot(xc.astype(bf16), bd))
    return jnp.concatenate(cols, axis=1)


def _store_groups(ref, x):
    for g in range(N_GROUPS):
        ref[g] = x[:, g * GROUP:(g + 1) * GROUP]


def _sigmoid(z):
    return 0.5 * jnp.tanh(0.5 * z) + 0.5


def _rwkv_prep_kernel(u_ref, h_ref, wg_ref,
                      w0f_ref, w2f_ref, w0b_ref, w2b_ref, a0f_ref, a2f_ref, a0b_ref, a2b_ref,
                      g2_ref, kkw_ref, kaw_ref, rk_ref, bd_ref,
                      r_ref, v_ref, kk_ref, kf_ref, kb_ref, bf_ref, bb_ref, lwf_ref, lwb_ref,
                      g_ref, bonus_ref, gates_ref):
    cw = 512
    h = h_ref[...]
    for c in range(2 * D_MODEL // cw):
        cs = slice(c * cw, (c + 1) * cw)
        gates_ref[:, cs] = _sigmoid(_dot(h, wg_ref[:, cs])).astype(bf16)

    d = D_MODEL
    r = u_ref[:, 0:d]
    k = u_ref[:, d:2 * d]
    v = u_ref[:, 2 * d:3 * d]
    xw_f = u_ref[:, 3 * d:3 * d + 128]
    xw_b = u_ref[:, 3 * d + 128:3 * d + 256]
    xa_f = u_ref[:, 3 * d + 256:3 * d + 384]
    xa_b = u_ref[:, 3 * d + 384:3 * d + 512]
    xg = u_ref[:, 3 * d + 512:U_PACKED]

    def neg_exp_logw(xw, w0_ref, w2_ref):
        z = w0_ref[...] + _dot(jnp.tanh(xw).astype(bf16), w2_ref[...])
        return (-np.exp(-0.5)) * _sigmoid(z)

    _store_groups(lwf_ref, neg_exp_logw(xw_f, w0f_ref, w2f_ref))
    _store_groups(lwb_ref, neg_exp_logw(xw_b, w0b_ref, w2b_ref))
    a_f = _sigmoid(a0f_ref[...] + _dot(xa_f.astype(bf16), a2f_ref[...]))
    a_b = _sigmoid(a0b_ref[...] + _dot(xa_b.astype(bf16), a2b_ref[...]))
    g_ref[...] = _dot(_sigmoid(xg).astype(bf16), g2_ref[...]).astype(bf16)

    bd = bd_ref[...]
    kk = k * kkw_ref[...]
    kk = kk * lax.rsqrt(jnp.maximum(_seg_sum(kk * kk, bd, split=False), 1e-24))
    ka = kaw_ref[...]
    k_f = k * (1.0 + (a_f - 1.0) * ka)
    k_b = k * (1.0 + (a_b - 1.0) * ka)
    bonus = _seg_sum(r * (k_f + k_b) * rk_ref[...], bd, split=False) * v

    _store_groups(r_ref, r.astype(bf16))
    _store_groups(v_ref, v.astype(bf16))
    _store_groups(kk_ref, kk.astype(bf16))
    _store_groups(kf_ref, k_f.astype(bf16))
    _store_groups(kb_ref, k_b.astype(bf16))
    _store_groups(bf_ref, (kk * a_f).astype(bf16))
    _store_groups(bb_ref, (kk * a_b).astype(bf16))
    bonus_ref[...] = bonus.astype(bf16)


def _rwkv_prep(u, h, w_g, vecs, loras, g2, bd, tm=TM_PREP):
    vec = _const_spec((1, D_MODEL))
    lora = _const_spec((LORA_PAD, D_MODEL))
    w0f, w0b, a0f, a0b, kkw, kaw, rk = vecs
    w2f, w2b, a2f, a2b = loras
    out_bf = jax.ShapeDtypeStruct((T_ALL, D_MODEL), bf16)
    grp_bf = jax.ShapeDtypeStruct((N_GROUPS, T_ALL, GROUP), bf16)
    grp_f = jax.ShapeDtypeStruct((N_GROUPS, T_ALL, GROUP), f32)
    row = pl.BlockSpec((tm, D_MODEL), lambda i: (i, 0))
    grp = pl.BlockSpec((N_GROUPS, tm, GROUP), lambda i: (0, i, 0))
    return pl.pallas_call(
        _rwkv_prep_kernel,
        grid=(T_ALL // tm,),
        in_specs=[pl.BlockSpec((tm, U_PACKED), lambda i: (i, 0)), row, _const_spec(w_g.shape),
                  vec, lora, vec, lora, vec, lora, vec, lora,
                  _const_spec((GATE_LORA, D_MODEL)), vec, vec, vec, _const_spec((GROUP, GROUP))],
        out_specs=[grp] * 9 + [row, row, pl.BlockSpec((tm, 2 * D_MODEL), lambda i: (i, 0))],
        out_shape=[grp_bf] * 7 + [grp_f, grp_f, out_bf, out_bf,
                                  jax.ShapeDtypeStruct((T_ALL, 2 * D_MODEL), bf16)],
        compiler_params=pltpu.CompilerParams(dimension_semantics=("parallel",),
                                             vmem_limit_bytes=VMEM_LIMIT),
        name="rwkv_prep_gates",
    )(u, h, w_g, w0f, w2f, w0b, w2b, a0f, a2f, a0b, a2b, g2, kkw, kaw, rk, bd)


def _wkv_masks(reverse):
    C = CHUNK
    t_i = lax.broadcasted_iota(jnp.int32, (C, GROUP), 0)
    s_i = lax.broadcasted_iota(jnp.int32, (C, GROUP), 1) % C
    strict = (s_i > t_i) if reverse else (s_i < t_i)
    incl = (s_i >= t_i) if reverse else (s_i <= t_i)
    eye = jnp.where(s_i == t_i, 1.0, 0.0)
    tt = lax.broadcasted_iota(jnp.int32, (C, C), 0)
    ss = lax.broadcasted_iota(jnp.int32, (C, C), 1)
    tri = jnp.where((ss >= tt) if reverse else (ss <= tt), 1.0, 0.0).astype(bf16)
    return strict, incl, eye, tri


def _wkv_chunk_operators(chains, bdmask):
    C = CHUNK
    n = range(len(chains))
    rs, ks, vs, kks, bs, lws, masks, revs = zip(*chains)
    strict = [m[0] for m in masks]
    incl = [m[1] for m in masks]
    eye = [m[2] for m in masks]
    tri = [m[3] for m in masks]
    last = [0 if rev else C - 1 for rev in revs]
    even_blk = (lax.broadcasted_iota(jnp.int32, (2 * C, GROUP), 1) // RWKV_HEAD) % 2 == 0

    def bd(x):
        return jnp.where(bdmask, jnp.concatenate([x] * 4, axis=0), 0.0).astype(bf16)

    def ss_mul(x, y_bd):
        return _dot(x.astype(bf16), y_bd)

    lw_hi = [lws[i].astype(bf16) for i in n]
    lw_lo = [(lws[i] - lw_hi[i].astype(f32)).astype(bf16) for i in n]
    cum = [_dot(tri[i], lw_hi[i]) + _dot(tri[i], lw_lo[i]) for i in n]
    w_inv = [jnp.exp(-cum[i]) for i in n]
    w_rem = [jnp.exp(cum[i][last[i]:last[i] + 1, :] - cum[i]) for i in n]
    a_t = [-kks[i] * jnp.exp(cum[i] - lws[i]) for i in n]
    r_t = [rs[i] * jnp.exp(cum[i]) for i in n]

    lhs = [jnp.concatenate([a_t[i], r_t[i]], axis=0).astype(bf16) for i in n]
    b_i = [bs[i] * w_inv[i] for i in n]
    k_i = [ks[i] * w_inv[i] for i in n]
    bk_t = [jnp.transpose(jnp.concatenate([b_i[i], k_i[i]], axis=0)) for i in n]
    kb_t = [jnp.transpose(jnp.concatenate([k_i[i], b_i[i]], axis=0)) for i in n]
    mix1 = [_dot(lhs[i], jnp.where(bdmask, jnp.concatenate([bk_t[i]] * 2, axis=1), 0.0).astype(bf16))
            for i in n]
    mix2 = [_dot(lhs[i], jnp.where(bdmask, jnp.concatenate([kb_t[i]] * 2, axis=1), 0.0).astype(bf16))
            for i in n]
    ab = [jnp.where(even_blk, mix1[i], mix2[i]) for i in n]
    ak = [jnp.where(even_blk, mix2[i], mix1[i]) for i in n]
    a_ab = [jnp.where(strict[i], ab[i][:C], 0.0) for i in n]
    a_rb = [jnp.where(incl[i], ab[i][C:], 0.0) for i in n]
    a_k = [jnp.concatenate([jnp.where(strict[i], ak[i][:C], 0.0),
                            jnp.where(incl[i], ak[i][C:], 0.0)], axis=0) for i in n]

    tinv = [eye[i] + a_ab[i] for i in n]
    pw = [ss_mul(a_ab[i], bd(a_ab[i])) for i in n]
    for _ in range(4):
        both = [ss_mul(jnp.concatenate([tinv[i], pw[i]], axis=0), bd(pw[i])) for i in n]
        tinv = [tinv[i] + both[i][:C] for i in n]
        pw = [both[i][C:] for i in n]
    tinv = [tinv[i] + ss_mul(tinv[i], bd(pw[i])) for i in n]

    zr = [ss_mul(a_k[i], bd(vs[i])) for i in n]
    pq = [ss_mul(tinv[i], jnp.concatenate([bd(a_t[i]), bd(zr[i][:C])], axis=1)) for i in n]
    p = [pq[i][:, :GROUP] for i in n]
    q = [pq[i][:, GROUP:] for i in n]
    yy = [ss_mul(a_rb[i], jnp.concatenate([bd(p[i]), bd(q[i])], axis=1)) for i in n]
    y_op = [r_t[i] + yy[i][:, :GROUP] for i in n]
    y_c = [yy[i][:, GROUP:] + zr[i][C:] for i in n]
    py = [jnp.concatenate([p[i], y_op[i]], axis=0).astype(bf16) for i in n]
    bkt = [jnp.transpose(jnp.concatenate([bs[i] * w_rem[i], ks[i] * w_rem[i]], axis=0)).astype(bf16)
           for i in n]
    w_tot = [jnp.broadcast_to(jnp.exp(jnp.transpose(cum[i])[:, last[i]:last[i] + 1]), (GROUP, 128))
             for i in n]
    return [(py[i], q[i], y_c[i], bkt[i], w_tot[i]) for i in n]


def _wkv_kernel(rf_ref, vf_ref, kkf_ref, kf_ref, bf_ref, lwf_ref,
                rb_ref, vb_ref, kkb_ref, kb_ref, bb_ref, lwb_ref,
                yf_ref, yb_ref,
                h_ref, py_ref, q_ref, yc_ref, bkt_ref, wt_ref, *, cb, ng):
    C = CHUNK

    c = pl.program_id(1)
    for d, edge in enumerate((c * (cb * C), (pl.num_programs(1) - c) * (cb * C))):
        @pl.when(_is_sequence_edge(edge))
        def _():
            for g in range(ng):
                h_ref[g, d] = jnp.zeros((GROUP, GROUP), f32)

    br = lax.broadcasted_iota(jnp.int32, (GROUP, GROUP), 0) // RWKV_HEAD
    bc = lax.broadcasted_iota(jnp.int32, (GROUP, GROUP), 1) // RWKV_HEAD
    bdmask = br == bc
    dirs = ((rf_ref, vf_ref, kkf_ref, kf_ref, bf_ref, lwf_ref, yf_ref, _wkv_masks(False)),
            (rb_ref, vb_ref, kkb_ref, kb_ref, bb_ref, lwb_ref, yb_ref, _wkv_masks(True)))

    def operators(gp, carry):
        chains, slots = [], []
        for g in (2 * gp, 2 * gp + 1):
            for j in range(cb):
                rows = pl.ds(j * C, C)
                for d, (r_ref, v_ref, kk_ref, k_ref, b_ref, lw_ref, _, masks) in enumerate(dirs):
                    chains.append((r_ref[g, rows, :].astype(f32), k_ref[g, rows, :].astype(f32),
                                   v_ref[g, rows, :].astype(f32), kk_ref[g, rows, :].astype(f32),
                                   b_ref[g, rows, :].astype(f32), lw_ref[g, rows, :], masks, d == 1))
                    slots.append((g, d, j))
        for (g, d, j), (py, q, y_c, bkt, wt) in zip(slots, _wkv_chunk_operators(chains, bdmask)):
            py_ref[g, d, j] = py
            q_ref[g, d, j] = q
            yc_ref[g, d, j] = y_c
            bkt_ref[g, d, j] = bkt
            wt_ref[g, d, j] = wt
        return carry

    lax.fori_loop(0, ng // 2, operators, 0)

    def recur(j, carry):
        jd = (j, cb - 1 - j)
        rows = [pl.ds(pl.multiple_of(jd[d] * C, C), C) for d in range(2)]
        gd = [(g, d) for g in range(ng) for d in range(2)]
        h = [h_ref[g, d] for g, d in gd]
        uy = [_dot(py_ref[g, d, jd[d]], h[i].astype(bf16)) for i, (g, d) in enumerate(gd)]
        uv = []
        for i, (g, d) in enumerate(gd):
            dirs[d][6][g, rows[d], :] = uy[i][C:] + yc_ref[g, d, jd[d]]
            u = uy[i][:C] + q_ref[g, d, jd[d]]
            uv.append(jnp.concatenate([u.astype(bf16), dirs[d][1][g, rows[d], :]], axis=0))
        upd = [_dot(bkt_ref[g, d, jd[d]], uv[i]) for i, (g, d) in enumerate(gd)]
        for i, (g, d) in enumerate(gd):
            wt = wt_ref[g, d, jd[d]]
            h_ref[g, d] = h[i] * jnp.concatenate([wt, wt], axis=1) + jnp.where(bdmask, upd[i], 0.0)
        return carry

    lax.fori_loop(0, cb, recur, 0)


def _wkv_scan(r, v, kk, k_f, b_f, lw_f, k_b, b_b, lw_b, cb=WKV_CHUNKS, ng=WKV_GROUPS):
    C = CHUNK
    rb = cb * C
    nblk = T_ALL // rb
    fwd = pl.BlockSpec((ng, rb, GROUP), lambda g, c: (g, c, 0))
    bwd = pl.BlockSpec((ng, rb, GROUP), lambda g, c: (g, nblk - 1 - c, 0))
    out = jax.ShapeDtypeStruct((N_GROUPS, T_ALL, GROUP), f32)
    return pl.pallas_call(
        functools.partial(_wkv_kernel, cb=cb, ng=ng),
        grid=(N_GROUPS // ng, nblk),
        in_specs=[fwd] * 6 + [bwd] * 6,
        out_specs=[fwd, bwd],
        out_shape=[out, out],
        scratch_shapes=[pltpu.VMEM((ng, 2, GROUP, GROUP), f32),
                        pltpu.VMEM((ng, 2, cb, 2 * C, GROUP), bf16),
                        pltpu.VMEM((ng, 2, cb, C, GROUP), f32),
                        pltpu.VMEM((ng, 2, cb, C, GROUP), f32),
                        pltpu.VMEM((ng, 2, cb, GROUP, 2 * C), bf16),
                        pltpu.VMEM((ng, 2, cb, GROUP, 128), f32)],
        compiler_params=pltpu.CompilerParams(
            dimension_semantics=("parallel", "arbitrary"),
            vmem_limit_bytes=VMEM_LIMIT),
        name="wkv_scan",
    )(r, v, kk, k_f, b_f, lw_f, r, v, kk, k_b, b_b, lw_b)


def _branch_kernel(yap_ref, yas_ref, yf_ref, yb_ref, bonus_ref, g_ref, g1_ref, g2_ref, lnw_ref, lnb_ref,
                   bd_ref, wa_ref, wb_ref, o_ref):
    bd = bd_ref[...]
    y = jnp.concatenate([yf_ref[g] + yb_ref[g] for g in range(N_GROUPS)], axis=1)
    inv_n = 1.0 / RWKV_HEAD
    mean = _seg_sum(y, bd) * inv_n
    dlt = y - mean
    var = _seg_sum(dlt * dlt, bd) * inv_n
    yn = dlt * lax.rsqrt(var + GN_EPS) * lnw_ref[...] + lnb_ref[...]
    yb = ((yn + bonus_ref[...].astype(f32)) * g_ref[...].astype(f32)).astype(bf16)
    ya = _pick_x(yap_ref, yas_ref)
    cw = 512
    for c in range(D_MODEL // cw):
        cs = slice(c * cw, (c + 1) * cw)
        oa = _dot(ya, wa_ref[:, cs])
        ob = _dot(yb, wb_ref[:, cs])
        o_ref[:, cs] = (g1_ref[:, cs].astype(f32) * oa + g2_ref[:, cs].astype(f32) * ob).astype(bf16)


def _branch(ya_p, ya_s, yf, yb, bonus, g, gates, lnw, lnb, bd, wa, wb, tm=TM_BRANCH):
    row = pl.BlockSpec((tm, D_MODEL), lambda i: (i, 0))
    grp = pl.BlockSpec((N_GROUPS, tm, GROUP), lambda i: (0, i, 0))
    return pl.pallas_call(
        _branch_kernel,
        grid=(T_ALL // tm,),
        in_specs=_x_specs(tm) + [grp, grp, row, row,
                  pl.BlockSpec((tm, D_MODEL), lambda i: (i, 0)),
                  pl.BlockSpec((tm, D_MODEL), lambda i: (i, 1)),
                  _const_spec((1, D_MODEL)), _const_spec((1, D_MODEL)), _const_spec((GROUP, GROUP)),
                  _const_spec((D_MODEL, D_MODEL)), _const_spec((D_MODEL, D_MODEL))],
        out_specs=row,
        out_shape=jax.ShapeDtypeStruct((T_ALL, D_MODEL), bf16),
        compiler_params=pltpu.CompilerParams(dimension_semantics=("parallel",),
                                             vmem_limit_bytes=VMEM_LIMIT),
        name="branch_mix",
    )(ya_p, ya_s, yf, yb, bonus, g, gates, gates, lnw, lnb, bd, wa, wb)


def _out_proj_kernel(o_ref, xp_ref, xs_ref, w_ref, g_ref, x1_ref):
    mix = _dot(o_ref[...], w_ref[...])
    x1_ref[...] = _pick_x(xp_ref, xs_ref) + _rms(mix, g_ref[...])


def _out_proj(o, xp, xs, w, g, tm=TM_OUT_PROJ):
    row = pl.BlockSpec((tm, D_MODEL), lambda i: (i, 0))
    return pl.pallas_call(
        _out_proj_kernel,
        grid=(T_ALL // tm,),
        in_specs=[row] + _x_specs(tm) + [_const_spec((D_MODEL, D_MODEL)), _const_spec((1, D_MODEL))],
        out_specs=row,
        out_shape=jax.ShapeDtypeStruct((T_ALL, D_MODEL), f32),
        compiler_params=pltpu.CompilerParams(dimension_semantics=("parallel",),
                                             vmem_limit_bytes=VMEM_LIMIT),
        name="out_proj",
    )(o, xp, xs, w, g)


def _mlp_kernel(x_ref, gpre_ref, wu_ref, wd_ref, gpost_ref, op_ref, os_ref, xn_ref, acc_ref):
    i = pl.program_id(0)
    j = pl.program_id(1)
    n_prompt_tiles = T_PROMPT // x_ref.shape[0]

    @pl.when(j == 0)
    def _():
        xn_ref[...] = _rms(x_ref[...], gpre_ref[...]).astype(bf16)
        acc_ref[...] = jnp.zeros_like(acc_ref)

    h = jnp.maximum(_dot(xn_ref[...], wu_ref[...]), 0.0)
    acc_ref[...] += _dot((h * h).astype(bf16), wd_ref[...])

    last = j == pl.num_programs(1) - 1

    @pl.when(jnp.logical_and(last, i < n_prompt_tiles))
    def _():
        op_ref[...] = x_ref[...] + _rms(acc_ref[...], gpost_ref[...])

    @pl.when(jnp.logical_and(last, i >= n_prompt_tiles))
    def _():
        os_ref[...] = x_ref[...] + _rms(acc_ref[...], gpost_ref[...])


def _mlp(x1, gpre, w_up, w_down, gpost, tm=TM_MLP, tf=TF_MLP):
    npb = T_PROMPT // tm
    row = pl.BlockSpec((tm, D_MODEL), lambda i, j: (i, 0))
    out = jax.ShapeDtypeStruct((T_PROMPT, D_MODEL), f32)
    return pl.pallas_call(
        _mlp_kernel,
        grid=(T_ALL // tm, D_FF // tf),
        in_specs=[row, _const_spec((1, D_MODEL)),
                  pl.BlockSpec((D_MODEL, tf), lambda i, j: (0, j)),
                  pl.BlockSpec((tf, D_MODEL), lambda i, j: (j, 0)),
                  _const_spec((1, D_MODEL))],
        out_specs=[pl.BlockSpec((tm, D_MODEL), lambda i, j: (jnp.minimum(i, npb - 1), 0)),
                   pl.BlockSpec((tm, D_MODEL), lambda i, j: (jnp.maximum(i - npb, 0), 0))],
        out_shape=[out, out],
        scratch_shapes=[pltpu.VMEM((tm, D_MODEL), bf16), pltpu.VMEM((tm, D_MODEL), f32)],
        compiler_params=pltpu.CompilerParams(dimension_semantics=("arbitrary", "arbitrary"),
                                             vmem_limit_bytes=VMEM_LIMIT),
        name="mlp",
    )(x1, gpre, w_up, w_down, gpost)


def _pad_lanes(a, width):
    return jnp.pad(a, [(0, 0)] * (a.ndim - 1) + [(0, width - a.shape[-1])])


def _pack_u(a):
    d = D_MODEL
    cuts = [3 * d, 3 * d + 96, 3 * d + 192, 3 * d + 288, 3 * d + 384]
    return jnp.concatenate(
        [a[..., :cuts[0]]]
        + [_pad_lanes(a[..., cuts[i]:cuts[i + 1]], LORA_PAD) for i in range(4)]
        + [a[..., cuts[4]:]], axis=-1)


def _pad_rows(w):
    return jnp.pad(w, [(0, LORA_PAD - w.shape[0]), (0, 0)])


def _rope_tables():
    inv = 1.0 / (ROPE_THETA ** (jnp.arange(0, ROPE, 2, dtype=f32) / ROPE))
    pos = jnp.concatenate([jnp.arange(T_PROMPT, dtype=f32),
                           jnp.tile(jnp.arange(S_SAMPLE, dtype=f32), B_SAMPLE)])
    ang = pos[:, None] * inv[None, :]
    cos, sin = jnp.cos(ang), jnp.sin(ang)
    cos = jnp.concatenate([cos, cos] * 2, axis=1)
    sin = jnp.concatenate([-sin, sin] * 2, axis=1)
    return cos, sin


def kernel(x_prompt, x_sample, norm_pre_mix, w_in, mu_prev, mu_next, mla_q_norm, mla_w_uq, mla_kv_norm, mla_w_ukv, rwkv_w0_f, rwkv_w2_f, rwkv_w0_b, rwkv_w2_b, rwkv_a0_f, rwkv_a2_f, rwkv_a0_b, rwkv_a2_b, rwkv_g2, rwkv_k_k, rwkv_k_a, rwkv_r_k, rwkv_ln_w, rwkv_ln_b, w_branch, w_out, norm_post_mix, norm_pre_mlp, w_mlp_up, w_mlp_down, norm_post_mlp):
    xp = x_prompt.reshape(T_PROMPT, D_MODEL)
    xs = x_sample.reshape(B_SAMPLE * S_SAMPLE, D_MODEL)

    swap = np.concatenate([np.arange(ROPE // 2, ROPE), np.arange(ROPE // 2)])
    w = w_in[0].astype(bf16)
    mla_in = Q_LORA + KV_LORA + ROPE
    rwkv_in = 3 * D_MODEL + 4 * 96 + GATE_LORA
    w_kr = w[:, Q_LORA + KV_LORA:mla_in]
    w_krs = w_kr[:, swap]
    w_mla = jnp.concatenate([w[:, :Q_LORA + KV_LORA], w_kr, w_kr, w_krs, w_krs], axis=1)
    w_u = _pack_u(w[:, mla_in:mla_in + rwkv_in])
    w_g = w[:, mla_in + rwkv_in:]
    mup = _pack_u(mu_prev)
    mun = _pack_u(mu_next)

    wq3 = mla_w_uq[0].reshape(Q_LORA, MLA_HEADS, QK_DIM)
    wq_rope = wq3[:, :, NOPE:]
    w_q = jnp.concatenate([wq3[:, :, :NOPE].reshape(Q_LORA, -1),
                           wq_rope.reshape(Q_LORA, -1),
                           wq_rope[:, :, swap].reshape(Q_LORA, -1)], axis=1).astype(bf16)
    wkv3 = mla_w_ukv[0].reshape(KV_LORA, MLA_HEADS, NOPE + V_DIM)
    w_kv = jnp.concatenate([wkv3[:, :, :NOPE].reshape(KV_LORA, -1),
                            wkv3[:, :, NOPE:].reshape(KV_LORA, -1)], axis=1).astype(bf16)
    cos, sin = _rope_tables()

    qn, qr, kn, v, kr, h = _mla_front(xp, xs, norm_pre_mix, w_mla, mla_q_norm, w_q, mla_kv_norm, w_kv,
                                      cos, sin)
    ya_p = _attention(qn, qr, kn, kr, v, 0, T_PROMPT, 1, unroll=4)
    ya_s = _attention(qn, qr, kn, kr, v, T_PROMPT, S_SAMPLE, B_SAMPLE, unroll=2)

    u = _proj_shift(h, w_u, 1.0 - mup - mun, mup, mun, "proj_rwkv")
    head_blk = np.arange(GROUP) // RWKV_HEAD
    bd = jnp.asarray(head_blk[:, None] == head_blk[None, :], dtype=bf16)
    vecs = (rwkv_w0_f, rwkv_w0_b, rwkv_a0_f, rwkv_a0_b, rwkv_k_k, rwkv_k_a,
            rwkv_r_k.reshape(1, D_MODEL))
    loras = tuple(_pad_rows(a[0]).astype(bf16) for a in (rwkv_w2_f, rwkv_w2_b, rwkv_a2_f, rwkv_a2_b))
    (r, vv, kk, k_f, k_b, b_f, b_b, lw_f, lw_b, g, bonus, gates) = _rwkv_prep(
        u, h, w_g, vecs, loras, rwkv_g2[0].astype(bf16), bd)
    y_f, y_b = _wkv_scan(r, vv, kk, k_f, b_f, lw_f, k_b, b_b, lw_b)

    wb = w_branch[0].astype(bf16)
    o = _branch(ya_p, ya_s, y_f, y_b, bonus, g, gates, rwkv_ln_w, rwkv_ln_b, bd,
                wb[:D_MODEL], wb[D_MODEL:])
    x1 = _out_proj(o, xp, xs, w_out[0].astype(bf16), norm_post_mix)
    out_p, out_s = _mlp(x1, norm_pre_mlp, w_mlp_up[0].astype(bf16), w_mlp_down[0].astype(bf16),
                        norm_post_mlp)
    return (out_p.reshape(1, T_PROMPT, D_MODEL), out_s.reshape(B_SAMPLE, S_SAMPLE, D_MODEL))
```

```python
import functools

import numpy as np
import jax
import jax.numpy as jnp
from jax import lax
from jax.experimental import pallas as pl
from jax.experimental.pallas import tpu as pltpu

f32 = jnp.float32
bf16 = jnp.bfloat16

D_MODEL = 2048
T_PROMPT = 8192
S_SAMPLE = 2048
B_SAMPLE = 4
T_ALL = T_PROMPT + B_SAMPLE * S_SAMPLE

MLA_HEADS = 16
Q_LORA = 768
KV_LORA = 512
NOPE = 128
ROPE = 64
V_DIM = 128
QK_DIM = NOPE + ROPE
ROPE_THETA = 10000.0
LOG2_E = 1.4426950408889634

RWKV_HEAD = 64
RWKV_HEADS = 32
LORA_PAD = 128
GATE_LORA = 256
U_PACKED = 3 * D_MODEL + 4 * LORA_PAD + GATE_LORA

D_FF = 4 * D_MODEL
NORM_EPS = 1e-6
GN_EPS = 64e-5

CHUNK = 64
GROUP = 256
N_GROUPS = D_MODEL // GROUP

VMEM_LIMIT = 56 * 1024 * 1024
TM_MLA_FRONT = 512
TM_PROJ, TN_PROJ = 1024, 768
TQ_ATTN_PROMPT, TQ_ATTN_SAMPLE, TK_ATTN = 512, 1024, 1024
TM_PREP = 128
WKV_CHUNKS, WKV_GROUPS = 4, 4
TM_BRANCH = 256
TM_OUT_PROJ = 512
TM_MLP, TF_MLP = 512, 1024


def _rms(x, g):
    return x * lax.rsqrt(jnp.mean(x * x, axis=-1, keepdims=True) + NORM_EPS) * g


def _dot(a, b):
    return jnp.dot(a, b, preferred_element_type=f32)


def _dot_nt(a, b):
    return lax.dot_general(a, b, (((1,), (1,)), ((), ())), preferred_element_type=f32)


def _split_dot(x, w):
    hi = x.astype(bf16)
    lo = (x - hi.astype(f32)).astype(bf16)
    return _dot(hi, w) + _dot(lo, w)


def _const_spec(shape):
    return pl.BlockSpec(shape, lambda *_: (0,) * len(shape), pipeline_mode=pl.Buffered(1))


def _x_specs(tm):
    npb = T_PROMPT // tm
    return [pl.BlockSpec((tm, D_MODEL), lambda i, *_: (jnp.minimum(i, npb - 1), 0)),
            pl.BlockSpec((tm, D_MODEL), lambda i, *_: (jnp.maximum(i - npb, 0), 0))]


def _is_sequence_edge(row):
    return jnp.logical_or(row == 0, jnp.logical_and(row >= T_PROMPT, (row - T_PROMPT) % S_SAMPLE == 0))


def _pick_x(xp_ref, xs_ref):
    tm = xp_ref.shape[0]
    return jnp.where(pl.program_id(0) < T_PROMPT // tm, xp_ref[...], xs_ref[...])


def _mla_front_kernel(xp_ref, xs_ref, g_ref, wm_ref, qg_ref, wq_ref, kvg_ref, wkv_ref, cos_ref, sin_ref,
                      qn_ref, qr_ref, kn_ref, v_ref, kr_ref, h_ref):
    h = _rms(_pick_x(xp_ref, xs_ref), g_ref[...]).astype(bf16)
    h_ref[...] = h
    pm = _dot(h, wm_ref[...])
    cos = cos_ref[...]
    sin = sin_ref[...]
    kr_ref[...] = (pm[:, 1280:1408] * cos + pm[:, 1408:1536] * sin).astype(bf16)
    qin = _rms(pm[:, :Q_LORA], qg_ref[...]).astype(bf16)
    kvin = _rms(pm[:, Q_LORA:Q_LORA + KV_LORA], kvg_ref[...]).astype(bf16)
    scale = QK_DIM ** -0.5 * LOG2_E
    cw = 512
    for c in range(D_MODEL // cw):
        q = _dot(qin, wq_ref[:, c * cw:(c + 1) * cw])
        qn_ref[:, c * cw:(c + 1) * cw] = (q * scale).astype(bf16)
    cos4 = jnp.concatenate([cos] * (cw // 128), axis=1)
    sin4 = jnp.concatenate([sin] * (cw // 128), axis=1)
    nrope = MLA_HEADS * ROPE
    for c in range(nrope // cw):
        qa = _dot(qin, wq_ref[:, D_MODEL + c * cw:D_MODEL + (c + 1) * cw])
        qb = _dot(qin, wq_ref[:, D_MODEL + nrope + c * cw:D_MODEL + nrope + (c + 1) * cw])
        qr_ref[:, c * cw:(c + 1) * cw] = ((qa * cos4 + qb * sin4) * scale).astype(bf16)
    for c in range(D_MODEL // cw):
        kn_ref[:, c * cw:(c + 1) * cw] = _dot(kvin, wkv_ref[:, c * cw:(c + 1) * cw]).astype(bf16)
        v_ref[:, c * cw:(c + 1) * cw] = _dot(
            kvin, wkv_ref[:, D_MODEL + c * cw:D_MODEL + (c + 1) * cw]).astype(bf16)


def _mla_front(xp, xs, g, w_mla, qg, w_q, kvg, w_kv, cos, sin, tm=TM_MLA_FRONT):
    nq = w_q.shape[1]
    row = lambda w: pl.BlockSpec((tm, w), lambda i: (i, 0))
    return pl.pallas_call(
        _mla_front_kernel,
        grid=(T_ALL // tm,),
        in_specs=_x_specs(tm) + [_const_spec((1, D_MODEL)), _const_spec(w_mla.shape),
                  _const_spec((1, Q_LORA)), _const_spec((Q_LORA, nq)),
                  _const_spec((1, KV_LORA)), _const_spec(w_kv.shape), row(128), row(128)],
        out_specs=[row(D_MODEL), row(MLA_HEADS * ROPE), row(D_MODEL), row(D_MODEL), row(128),
                   row(D_MODEL)],
        out_shape=[jax.ShapeDtypeStruct((T_ALL, D_MODEL), bf16),
                   jax.ShapeDtypeStruct((T_ALL, MLA_HEADS * ROPE), bf16),
                   jax.ShapeDtypeStruct((T_ALL, D_MODEL), bf16),
                   jax.ShapeDtypeStruct((T_ALL, D_MODEL), bf16),
                   jax.ShapeDtypeStruct((T_ALL, 128), bf16),
                   jax.ShapeDtypeStruct((T_ALL, D_MODEL), bf16)],
        compiler_params=pltpu.CompilerParams(dimension_semantics=("parallel",),
                                             vmem_limit_bytes=VMEM_LIMIT),
        name="mla_front",
    )(xp, xs, g, w_mla, qg, w_q, kvg, w_kv, cos, sin)


HALO = 16


def _proj_shift_kernel(hp_ref, h_ref, hn_ref, w_ref, mu0_ref, mup_ref, mun_ref, o_ref):
    tm = h_ref.shape[0]
    t0 = pl.program_id(0) * tm
    keep_prev = jnp.where(_is_sequence_edge(t0), 0.0, 1.0)
    keep_next = jnp.where(_is_sequence_edge(t0 + tm), 0.0, 1.0)
    lhs = jnp.concatenate([hp_ref[...], h_ref[...], hn_ref[...]], axis=0)
    ue = _dot(lhs, w_ref[...])
    n = tm + 2 * HALO
    down = pltpu.roll(ue, 1, axis=0)[HALO:HALO + tm]
    up = pltpu.roll(ue, n - 1, axis=0)[HALO:HALO + tm]
    row8 = lax.broadcasted_iota(jnp.int32, (8, 1), 0)
    head = jnp.where(row8 == 0, down[:8] * keep_prev, down[:8])
    tail = jnp.where(row8 == 7, up[tm - 8:] * keep_next, up[tm - 8:])
    prev = jnp.concatenate([head, down[8:]], axis=0)
    nxt = jnp.concatenate([up[:tm - 8], tail], axis=0)
    o_ref[...] = mu0_ref[...] * ue[HALO:HALO + tm] + mup_ref[...] * prev + mun_ref[...] * nxt


def _proj_shift(h, w, mu0, mup, mun, name, tm=TM_PROJ, tn=TN_PROJ):
    n = w.shape[1]
    hb = tm // HALO
    nblk = T_ALL // HALO
    col = pl.BlockSpec((1, tn), lambda i, j: (0, j))
    return pl.pallas_call(
        _proj_shift_kernel,
        grid=(T_ALL // tm, n // tn),
        in_specs=[pl.BlockSpec((HALO, D_MODEL), lambda i, j: (jnp.maximum(i * hb - 1, 0), 0)),
                  pl.BlockSpec((tm, D_MODEL), lambda i, j: (i, 0)),
                  pl.BlockSpec((HALO, D_MODEL), lambda i, j: (jnp.minimum((i + 1) * hb, nblk - 1), 0)),
                  pl.BlockSpec((D_MODEL, tn), lambda i, j: (0, j)), col, col, col],
        out_specs=pl.BlockSpec((tm, tn), lambda i, j: (i, j)),
        out_shape=jax.ShapeDtypeStruct((T_ALL, n), f32),
        compiler_params=pltpu.CompilerParams(dimension_semantics=("parallel", "arbitrary"),
                                             vmem_limit_bytes=VMEM_LIMIT),
        name=name,
    )(h, h, h, w, mu0, mup, mun)


def _attn_kernel(qn_ref, qr_ref, kn_ref, kr_ref, v_ref, o_ref, kcat_ref, vt_ref, s_ref, p_ref, *,
                 seq, tq, tk, unroll):
    head = pl.program_id(1)
    nsteps = seq // tk

    @pl.when(pl.program_id(2) == 0)
    def _():
        kcat_ref[:, :NOPE] = kn_ref[...]
        kcat_ref[:, NOPE:] = kr_ref[...]
        for c in range(seq // 512):
            cs = slice(c * 512, (c + 1) * 512)
            vt_ref[:, cs] = jnp.transpose(v_ref[cs, :].astype(f32)).astype(bf16)

    lane = lax.broadcasted_iota(jnp.int32, (tq, 128), 1)
    keep = (lane // ROPE) == (head % 2)
    qr = jnp.where(keep, qr_ref[...].astype(f32), 0.0).astype(bf16)
    q = jnp.concatenate([qn_ref[...], qr], axis=1)

    def scores(j, slot):
        rows = pl.ds(pl.multiple_of(j * tk, tk), tk)
        s_ref[slot] = _dot_nt(kcat_ref[rows, :], q)

    def absorb(j, slot, carry):
        m, l8, acc = carry
        s = s_ref[slot]
        m_loc = jnp.max(s.reshape(tk // 8, 8, tq), axis=0)
        m_new = jnp.maximum(m, jnp.max(m_loc, axis=0, keepdims=True))
        alpha = jnp.exp2(m - m_new)
        l8 = alpha * l8
        rc = 256
        for c in range(tk // rc):
            p = jnp.exp2(s_ref[slot, c * rc:(c + 1) * rc, :] - m_new)
            l8 = l8 + jnp.sum(p.reshape(rc // 8, 8, tq), axis=0)
            p_ref[c * rc:(c + 1) * rc, :] = p.astype(bf16)
        cols = pl.ds(pl.multiple_of(j * tk, tk), tk)
        acc = alpha * acc + _dot(vt_ref[:, cols], p_ref[...])
        return m_new, l8, acc

    def run(j0, carry, is_tail):
        for u in range(unroll):
            if not (is_tail and u == unroll - 1):
                scores(j0 + u + 1, (u + 1) % 2)
            carry = absorb(j0 + u, u % 2, carry)
        return carry

    carry = (jnp.full((1, tq), -jnp.inf, f32), jnp.zeros((8, tq), f32), jnp.zeros((V_DIM, tq), f32))
    scores(0, 0)
    carry = lax.fori_loop(0, nsteps // unroll - 1, lambda i, c: run(i * unroll, c, False), carry)
    _, l8, acc = run(nsteps - unroll, carry, True)
    out_t = acc / jnp.sum(l8, axis=0, keepdims=True)
    o_ref[...] = jnp.transpose(out_t).astype(o_ref.dtype)


def _attention(qn, qr, kn, kr, v, row_off, seq, nb, unroll, tq, tk=TK_ATTN):
    qoff = row_off // tq
    soff = row_off // seq
    nq = seq // tq
    return pl.pallas_call(
        functools.partial(_attn_kernel, seq=seq, tq=tq, tk=tk, unroll=unroll),
        grid=(nb, MLA_HEADS, nq),
        in_specs=[pl.BlockSpec((tq, NOPE), lambda b, h, i: (qoff + b * nq + i, h)),
                  pl.BlockSpec((tq, 128), lambda b, h, i: (qoff + b * nq + i, h // 2)),
                  pl.BlockSpec((seq, NOPE), lambda b, h, i: (soff + b, h)),
                  pl.BlockSpec((seq, 128), lambda b, h, i: (soff + b, 0)),
                  pl.BlockSpec((seq, V_DIM), lambda b, h, i: (soff + b, h))],
        out_specs=pl.BlockSpec((tq, V_DIM), lambda b, h, i: (b * nq + i, h)),
        out_shape=jax.ShapeDtypeStruct((nb * seq, D_MODEL), bf16),
        scratch_shapes=[pltpu.VMEM((seq, 2 * NOPE), bf16), pltpu.VMEM((V_DIM, seq), bf16),
                        pltpu.VMEM((2, tk, tq), f32), pltpu.VMEM((tk, tq), bf16)],
        compiler_params=pltpu.CompilerParams(
            dimension_semantics=("parallel", "parallel", "arbitrary"),
            vmem_limit_bytes=VMEM_LIMIT),
        name=f"mla_attention_s{seq}",
    )(qn, qr, kn, kr, v)


def _seg_sum(x, bd, split=True):
    cols = []
    for c in range(x.shape[1] // GROUP):
        xc = x[:, c * GROUP:(c + 1) * GROUP]
        cols.append(_split_dot(xc, bd) if split else _dot(xc.astype(bf16), bd))
    return jnp.concatenate(cols, axis=1)


def _store_groups(ref, x):
    for g in range(N_GROUPS):
        ref[g] = x[:, g * GROUP:(g + 1) * GROUP]


def _sigmoid(z):
    return 0.5 * jnp.tanh(0.5 * z) + 0.5


def _rwkv_prep_kernel(u_ref, h_ref, wg_ref,
                      w0f_ref, w2f_ref, w0b_ref, w2b_ref, a0f_ref, a2f_ref, a0b_ref, a2b_ref,
                      g2_ref, kkw_ref, kaw_ref, rk_ref, bd_ref,
                      r_ref, v_ref, kk_ref, kf_ref, kb_ref, bf_ref, bb_ref, lwf_ref, lwb_ref,
                      g_ref, bonus_ref, gates_ref):
    cw = 512
    h = h_ref[...]
    for c in range(2 * D_MODEL // cw):
        cs = slice(c * cw, (c + 1) * cw)
        gates_ref[:, cs] = _sigmoid(_dot(h, wg_ref[:, cs])).astype(bf16)

    d = D_MODEL
    r = u_ref[:, 0:d]
    k = u_ref[:, d:2 * d]
    v = u_ref[:, 2 * d:3 * d]
    xw_f = u_ref[:, 3 * d:3 * d + 128]
    xw_b = u_ref[:, 3 * d + 128:3 * d + 256]
    xa_f = u_ref[:, 3 * d + 256:3 * d + 384]
    xa_b = u_ref[:, 3 * d + 384:3 * d + 512]
    xg = u_ref[:, 3 * d + 512:U_PACKED]

    def neg_exp_logw(xw, w0_ref, w2_ref):
        z = w0_ref[...] + _dot(jnp.tanh(xw).astype(bf16), w2_ref[...])
        return (-np.exp(-0.5)) * _sigmoid(z)

    _store_groups(lwf_ref, neg_exp_logw(xw_f, w0f_ref, w2f_ref))
    _store_groups(lwb_ref, neg_exp_logw(xw_b, w0b_ref, w2b_ref))
    a_f = _sigmoid(a0f_ref[...] + _dot(xa_f.astype(bf16), a2f_ref[...]))
    a_b = _sigmoid(a0b_ref[...] + _dot(xa_b.astype(bf16), a2b_ref[...]))
    g_ref[...] = _dot(_sigmoid(xg).astype(bf16), g2_ref[...]).astype(bf16)

    bd = bd_ref[...]
    kk = k * kkw_ref[...]
    kk = kk * lax.rsqrt(jnp.maximum(_seg_sum(kk * kk, bd, split=False), 1e-24))
    ka = kaw_ref[...]
    k_f = k * (1.0 + (a_f - 1.0) * ka)
    k_b = k * (1.0 + (a_b - 1.0) * ka)
    bonus = _seg_sum(r * (k_f + k_b) * rk_ref[...], bd, split=False) * v

    _store_groups(r_ref, r.astype(bf16))
    _store_groups(v_ref, v.astype(bf16))
    _store_groups(kk_ref, kk.astype(bf16))
    _store_groups(kf_ref, k_f.astype(bf16))
    _store_groups(kb_ref, k_b.astype(bf16))
    _store_groups(bf_ref, (kk * a_f).astype(bf16))
    _store_groups(bb_ref, (kk * a_b).astype(bf16))
    bonus_ref[...] = bonus.astype(bf16)


def _rwkv_prep(u, h, w_g, vecs, loras, g2, bd, tm=TM_PREP):
    vec = _const_spec((1, D_MODEL))
    lora = _const_spec((LORA_PAD, D_MODEL))
    w0f, w0b, a0f, a0b, kkw, kaw, rk = vecs
    w2f, w2b, a2f, a2b = loras
    out_bf = jax.ShapeDtypeStruct((T_ALL, D_MODEL), bf16)
    grp_bf = jax.ShapeDtypeStruct((N_GROUPS, T_ALL, GROUP), bf16)
    grp_f = jax.ShapeDtypeStruct((N_GROUPS, T_ALL, GROUP), f32)
    row = pl.BlockSpec((tm, D_MODEL), lambda i: (i, 0))
    grp = pl.BlockSpec((N_GROUPS, tm, GROUP), lambda i: (0, i, 0))
    return pl.pallas_call(
        _rwkv_prep_kernel,
        grid=(T_ALL // tm,),
        in_specs=[pl.BlockSpec((tm, U_PACKED), lambda i: (i, 0)), row, _const_spec(w_g.shape),
                  vec, lora, vec, lora, vec, lora, vec, lora,
                  _const_spec((GATE_LORA, D_MODEL)), vec, vec, vec, _const_spec((GROUP, GROUP))],
        out_specs=[grp] * 9 + [row, row, pl.BlockSpec((tm, 2 * D_MODEL), lambda i: (i, 0))],
        out_shape=[grp_bf] * 7 + [grp_f, grp_f, out_bf, out_bf,
                                  jax.ShapeDtypeStruct((T_ALL, 2 * D_MODEL), bf16)],
        compiler_params=pltpu.CompilerParams(dimension_semantics=("parallel",),
                                             vmem_limit_bytes=VMEM_LIMIT),
        name="rwkv_prep_gates",
    )(u, h, w_g, w0f, w2f, w0b, w2b, a0f, a2f, a0b, a2b, g2, kkw, kaw, rk, bd)


def _wkv_masks(reverse):
    C = CHUNK
    t_i = lax.broadcasted_iota(jnp.int32, (C, GROUP), 0)
    s_i = lax.broadcasted_iota(jnp.int32, (C, GROUP), 1) % C
    strict = (s_i > t_i) if reverse else (s_i < t_i)
    incl = (s_i >= t_i) if reverse else (s_i <= t_i)
    eye = jnp.where(s_i == t_i, 1.0, 0.0)
    tt = lax.broadcasted_iota(jnp.int32, (C, C), 0)
    ss = lax.broadcasted_iota(jnp.int32, (C, C), 1)
    tri = jnp.where((ss >= tt) if reverse else (ss <= tt), 1.0, 0.0).astype(bf16)
    return strict, incl, eye, tri


def _wkv_chunk_operators(chains, bdmask):
    C = CHUNK
    n = range(len(chains))
    rs, ks, vs, kks, bs, lws, masks, revs = zip(*chains)
    strict = [m[0] for m in masks]
    incl = [m[1] for m in masks]
    eye = [m[2] for m in masks]
    tri = [m[3] for m in masks]
    last = [0 if rev else C - 1 for rev in revs]
    even_blk = (lax.broadcasted_iota(jnp.int32, (2 * C, GROUP), 1) // RWKV_HEAD) % 2 == 0

    def bd(x):
        return jnp.where(bdmask, jnp.concatenate([x] * 4, axis=0), 0.0).astype(bf16)

    def ss_mul(x, y_bd):
        return _dot(x.astype(bf16), y_bd)

    lw_hi = [lws[i].astype(bf16) for i in n]
    lw_lo = [(lws[i] - lw_hi[i].astype(f32)).astype(bf16) for i in n]
    cum = [_dot(tri[i], lw_hi[i]) + _dot(tri[i], lw_lo[i]) for i in n]
    w_inv = [jnp.exp(-cum[i]) for i in n]
    w_rem = [jnp.exp(cum[i][last[i]:last[i] + 1, :] - cum[i]) for i in n]
    a_t = [-kks[i] * jnp.exp(cum[i] - lws[i]) for i in n]
    r_t = [rs[i] * jnp.exp(cum[i]) for i in n]

    lhs = [jnp.concatenate([a_t[i], r_t[i]], axis=0).astype(bf16) for i in n]
    b_i = [bs[i] * w_inv[i] for i in n]
    k_i = [ks[i] * w_inv[i] for i in n]
    bk_t = [jnp.transpose(jnp.concatenate([b_i[i], k_i[i]], axis=0)) for i in n]
    kb_t = [jnp.transpose(jnp.concatenate([k_i[i], b_i[i]], axis=0)) for i in n]
    mix1 = [_dot(lhs[i], jnp.where(bdmask, jnp.concatenate([bk_t[i]] * 2, axis=1), 0.0).astype(bf16))
            for i in n]
    mix2 = [_dot(lhs[i], jnp.where(bdmask, jnp.concatenate([kb_t[i]] * 2, axis=1), 0.0).astype(bf16))
            for i in n]
    ab = [jnp.where(even_blk, mix1[i], mix2[i]) for i in n]
    ak = [jnp.where(even_blk, mix2[i], mix1[i]) for i in n]
    a_ab = [jnp.where(strict[i], ab[i][:C], 0.0) for i in n]
    a_rb = [jnp.where(incl[i], ab[i][C:], 0.0) for i in n]
    a_k = [jnp.concatenate([jnp.where(strict[i], ak[i][:C], 0.0),
                            jnp.where(incl[i], ak[i][C:], 0.0)], axis=0) for i in n]

    tinv = [eye[i] + a_ab[i] for i in n]
    pw = [ss_mul(a_ab[i], bd(a_ab[i])) for i in n]
    for _ in range(4):
        both = [ss_mul(jnp.concatenate([tinv[i], pw[i]], axis=0), bd(pw[i])) for i in n]
        tinv = [tinv[i] + both[i][:C] for i in n]
        pw = [both[i][C:] for i in n]
    tinv = [tinv[i] + ss_mul(tinv[i], bd(pw[i])) for i in n]

    zr = [ss_mul(a_k[i], bd(vs[i])) for i in n]
    art = [ss_mul(a_rb[i], bd(tinv[i])) for i in n]
    pqy = [ss_mul(jnp.concatenate([tinv[i], art[i]], axis=0),
                  jnp.concatenate([bd(a_t[i]), bd(zr[i][:C])], axis=1)) for i in n]
    p = [pqy[i][:C, :GROUP] for i in n]
    q = [pqy[i][:C, GROUP:] for i in n]
    y_op = [r_t[i] + pqy[i][C:, :GROUP] for i in n]
    y_c = [pqy[i][C:, GROUP:] + zr[i][C:] for i in n]
    py = [jnp.concatenate([p[i], y_op[i]], axis=0).astype(bf16) for i in n]
    bkt = [jnp.transpose(jnp.concatenate([bs[i] * w_rem[i], ks[i] * w_rem[i]], axis=0)).astype(bf16)
           for i in n]
    w_tot = [jnp.broadcast_to(jnp.exp(jnp.transpose(cum[i])[:, last[i]:last[i] + 1]), (GROUP, 128))
             for i in n]
    return [(py[i], q[i], y_c[i], bkt[i], w_tot[i]) for i in n]


def _wkv_kernel(rf_ref, vf_ref, kkf_ref, kf_ref, bf_ref, lwf_ref,
                rb_ref, vb_ref, kkb_ref, kb_ref, bb_ref, lwb_ref,
                yf_ref, yb_ref,
                h_ref, py_ref, q_ref, yc_ref, bkt_ref, wt_ref, *, cb, ng):
    C = CHUNK

    c = pl.program_id(1)
    for d, edge in enumerate((c * (cb * C), (pl.num_programs(1) - c) * (cb * C))):
        @pl.when(_is_sequence_edge(edge))
        def _():
            for g in range(ng):
                h_ref[g, d] = jnp.zeros((GROUP, GROUP), f32)

    br = lax.broadcasted_iota(jnp.int32, (GROUP, GROUP), 0) // RWKV_HEAD
    bc = lax.broadcasted_iota(jnp.int32, (GROUP, GROUP), 1) // RWKV_HEAD
    bdmask = br == bc
    dirs = ((rf_ref, vf_ref, kkf_ref, kf_ref, bf_ref, lwf_ref, yf_ref, _wkv_masks(False)),
            (rb_ref, vb_ref, kkb_ref, kb_ref, bb_ref, lwb_ref, yb_ref, _wkv_masks(True)))

    def operators(gp, carry):
        chains, slots = [], []
        for g in (2 * gp, 2 * gp + 1):
            for j in range(cb):
                rows = pl.ds(j * C, C)
                for d, (r_ref, v_ref, kk_ref, k_ref, b_ref, lw_ref, _, masks) in enumerate(dirs):
                    chains.append((r_ref[g, rows, :].astype(f32), k_ref[g, rows, :].astype(f32),
                                   v_ref[g, rows, :].astype(f32), kk_ref[g, rows, :].astype(f32),
                                   b_ref[g, rows, :].astype(f32), lw_ref[g, rows, :], masks, d == 1))
                    slots.append((g, d, j))
        for (g, d, j), (py, q, y_c, bkt, wt) in zip(slots, _wkv_chunk_operators(chains, bdmask)):
            py_ref[g, d, j] = py
            q_ref[g, d, j] = q
            yc_ref[g, d, j] = y_c
            bkt_ref[g, d, j] = bkt
            wt_ref[g, d, j] = wt
        return carry

    lax.fori_loop(0, ng // 2, operators, 0)

    def recur(j, carry):
        jd = (j, cb - 1 - j)
        rows = [pl.ds(pl.multiple_of(jd[d] * C, C), C) for d in range(2)]
        gd = [(g, d) for g in range(ng) for d in range(2)]
        h = [h_ref[g, d] for g, d in gd]
        uy = [_dot(py_ref[g, d, jd[d]], h[i].astype(bf16)) for i, (g, d) in enumerate(gd)]
        uv = []
        for i, (g, d) in enumerate(gd):
            dirs[d][6][g, rows[d], :] = uy[i][C:] + yc_ref[g, d, jd[d]]
            u = uy[i][:C] + q_ref[g, d, jd[d]]
            uv.append(jnp.concatenate([u.astype(bf16), dirs[d][1][g, rows[d], :]], axis=0))
        upd = [_dot(bkt_ref[g, d, jd[d]], uv[i]) for i, (g, d) in enumerate(gd)]
        for i, (g, d) in enumerate(gd):
            wt = wt_ref[g, d, jd[d]]
            h_ref[g, d] = h[i] * jnp.concatenate([wt, wt], axis=1) + jnp.where(bdmask, upd[i], 0.0)
        return carry

    lax.fori_loop(0, cb, recur, 0)


def _wkv_scan(r, v, kk, k_f, b_f, lw_f, k_b, b_b, lw_b, cb=WKV_CHUNKS, ng=WKV_GROUPS):
    C = CHUNK
    rb = cb * C
    nblk = T_ALL // rb
    fwd = pl.BlockSpec((ng, rb, GROUP), lambda g, c: (g, c, 0))
    bwd = pl.BlockSpec((ng, rb, GROUP), lambda g, c: (g, nblk - 1 - c, 0))
    out = jax.ShapeDtypeStruct((N_GROUPS, T_ALL, GROUP), f32)
    return pl.pallas_call(
        functools.partial(_wkv_kernel, cb=cb, ng=ng),
        grid=(N_GROUPS // ng, nblk),
        in_specs=[fwd] * 6 + [bwd] * 6,
        out_specs=[fwd, bwd],
        out_shape=[out, out],
        scratch_shapes=[pltpu.VMEM((ng, 2, GROUP, GROUP), f32),
                        pltpu.VMEM((ng, 2, cb, 2 * C, GROUP), bf16),
                        pltpu.VMEM((ng, 2, cb, C, GROUP), f32),
                        pltpu.VMEM((ng, 2, cb, C, GROUP), f32),
                        pltpu.VMEM((ng, 2, cb, GROUP, 2 * C), bf16),
                        pltpu.VMEM((ng, 2, cb, GROUP, 128), f32)],
        compiler_params=pltpu.CompilerParams(
            dimension_semantics=("parallel", "arbitrary"),
            vmem_limit_bytes=VMEM_LIMIT),
        name="wkv_scan",
    )(r, v, kk, k_f, b_f, lw_f, r, v, kk, k_b, b_b, lw_b)


def _branch_kernel(yap_ref, yas_ref, yf_ref, yb_ref, bonus_ref, g_ref, g1_ref, g2_ref, lnw_ref, lnb_ref,
                   bd_ref, wa_ref, wb_ref, o_ref):
    bd = bd_ref[...]
    y = jnp.concatenate([yf_ref[g] + yb_ref[g] for g in range(N_GROUPS)], axis=1)
    inv_n = 1.0 / RWKV_HEAD
    mean = _seg_sum(y, bd) * inv_n
    dlt = y - mean
    var = _seg_sum(dlt * dlt, bd) * inv_n
    yn = dlt * lax.rsqrt(var + GN_EPS) * lnw_ref[...] + lnb_ref[...]
    yb = ((yn + bonus_ref[...].astype(f32)) * g_ref[...].astype(f32)).astype(bf16)
    ya = _pick_x(yap_ref, yas_ref)
    cw = 512
    for c in range(D_MODEL // cw):
        cs = slice(c * cw, (c + 1) * cw)
        oa = _dot(ya, wa_ref[:, cs])
        ob = _dot(yb, wb_ref[:, cs])
        o_ref[:, cs] = (g1_ref[:, cs].astype(f32) * oa + g2_ref[:, cs].astype(f32) * ob).astype(bf16)


def _branch(ya_p, ya_s, yf, yb, bonus, g, gates, lnw, lnb, bd, wa, wb, tm=TM_BRANCH):
    row = pl.BlockSpec((tm, D_MODEL), lambda i: (i, 0))
    grp = pl.BlockSpec((N_GROUPS, tm, GROUP), lambda i: (0, i, 0))
    return pl.pallas_call(
        _branch_kernel,
        grid=(T_ALL // tm,),
        in_specs=_x_specs(tm) + [grp, grp, row, row,
                  pl.BlockSpec((tm, D_MODEL), lambda i: (i, 0)),
                  pl.BlockSpec((tm, D_MODEL), lambda i: (i, 1)),
                  _const_spec((1, D_MODEL)), _const_spec((1, D_MODEL)), _const_spec((GROUP, GROUP)),
                  _const_spec((D_MODEL, D_MODEL)), _const_spec((D_MODEL, D_MODEL))],
        out_specs=row,
        out_shape=jax.ShapeDtypeStruct((T_ALL, D_MODEL), bf16),
        compiler_params=pltpu.CompilerParams(dimension_semantics=("parallel",),
                                             vmem_limit_bytes=VMEM_LIMIT),
        name="branch_mix",
    )(ya_p, ya_s, yf, yb, bonus, g, gates, gates, lnw, lnb, bd, wa, wb)


def _out_proj_kernel(o_ref, xp_ref, xs_ref, w_ref, g_ref, x1_ref):
    mix = _dot(o_ref[...], w_ref[...])
    x1_ref[...] = _pick_x(xp_ref, xs_ref) + _rms(mix, g_ref[...])


def _out_proj(o, xp, xs, w, g, tm=TM_OUT_PROJ):
    row = pl.BlockSpec((tm, D_MODEL), lambda i: (i, 0))
    return pl.pallas_call(
        _out_proj_kernel,
        grid=(T_ALL // tm,),
        in_specs=[row] + _x_specs(tm) + [_const_spec((D_MODEL, D_MODEL)), _const_spec((1, D_MODEL))],
        out_specs=row,
        out_shape=jax.ShapeDtypeStruct((T_ALL, D_MODEL), f32),
        compiler_params=pltpu.CompilerParams(dimension_semantics=("parallel",),
                                             vmem_limit_bytes=VMEM_LIMIT),
        name="out_proj",
    )(o, xp, xs, w, g)


def _mlp_kernel(x_ref, gpre_ref, wu_ref, wd_ref, gpost_ref, op_ref, os_ref, xn_ref, acc_ref):
    i = pl.program_id(0)
    j = pl.program_id(1)
    n_prompt_tiles = T_PROMPT // x_ref.shape[0]

    @pl.when(j == 0)
    def _():
        xn_ref[...] = _rms(x_ref[...], gpre_ref[...]).astype(bf16)
        acc_ref[...] = jnp.zeros_like(acc_ref)

    h = jnp.maximum(_dot(xn_ref[...], wu_ref[...]), 0.0)
    acc_ref[...] += _dot((h * h).astype(bf16), wd_ref[...])

    last = j == pl.num_programs(1) - 1

    @pl.when(jnp.logical_and(last, i < n_prompt_tiles))
    def _():
        op_ref[...] = x_ref[...] + _rms(acc_ref[...], gpost_ref[...])

    @pl.when(jnp.logical_and(last, i >= n_prompt_tiles))
    def _():
        os_ref[...] = x_ref[...] + _rms(acc_ref[...], gpost_ref[...])


def _mlp(x1, gpre, w_up, w_down, gpost, tm=TM_MLP, tf=TF_MLP):
    npb = T_PROMPT // tm
    row = pl.BlockSpec((tm, D_MODEL), lambda i, j: (i, 0))
    out = jax.ShapeDtypeStruct((T_PROMPT, D_MODEL), f32)
    return pl.pallas_call(
        _mlp_kernel,
        grid=(T_ALL // tm, D_FF // tf),
        in_specs=[row, _const_spec((1, D_MODEL)),
                  pl.BlockSpec((D_MODEL, tf), lambda i, j: (0, j)),
                  pl.BlockSpec((tf, D_MODEL), lambda i, j: (j, 0)),
                  _const_spec((1, D_MODEL))],
        out_specs=[pl.BlockSpec((tm, D_MODEL), lambda i, j: (jnp.minimum(i, npb - 1), 0)),
                   pl.BlockSpec((tm, D_MODEL), lambda i, j: (jnp.maximum(i - npb, 0), 0))],
        out_shape=[out, out],
        scratch_shapes=[pltpu.VMEM((tm, D_MODEL), bf16), pltpu.VMEM((tm, D_MODEL), f32)],
        compiler_params=pltpu.CompilerParams(dimension_semantics=("arbitrary", "arbitrary"),
                                             vmem_limit_bytes=VMEM_LIMIT),
        name="mlp",
    )(x1, gpre, w_up, w_down, gpost)


def _pad_lanes(a, width):
    return jnp.pad(a, [(0, 0)] * (a.ndim - 1) + [(0, width - a.shape[-1])])


def _pack_u(a):
    d = D_MODEL
    cuts = [3 * d, 3 * d + 96, 3 * d + 192, 3 * d + 288, 3 * d + 384]
    return jnp.concatenate(
        [a[..., :cuts[0]]]
        + [_pad_lanes(a[..., cuts[i]:cuts[i + 1]], LORA_PAD) for i in range(4)]
        + [a[..., cuts[4]:]], axis=-1)


def _pad_rows(w):
    return jnp.pad(w, [(0, LORA_PAD - w.shape[0]), (0, 0)])


def _rope_tables():
    inv = 1.0 / (ROPE_THETA ** (jnp.arange(0, ROPE, 2, dtype=f32) / ROPE))
    pos = jnp.concatenate([jnp.arange(T_PROMPT, dtype=f32),
                           jnp.tile(jnp.arange(S_SAMPLE, dtype=f32), B_SAMPLE)])
    ang = pos[:, None] * inv[None, :]
    cos, sin = jnp.cos(ang), jnp.sin(ang)
    cos = jnp.concatenate([cos, cos] * 2, axis=1)
    sin = jnp.concatenate([-sin, sin] * 2, axis=1)
    return cos, sin


def kernel(x_prompt, x_sample, norm_pre_mix, w_in, mu_prev, mu_next, mla_q_norm, mla_w_uq, mla_kv_norm, mla_w_ukv, rwkv_w0_f, rwkv_w2_f, rwkv_w0_b, rwkv_w2_b, rwkv_a0_f, rwkv_a2_f, rwkv_a0_b, rwkv_a2_b, rwkv_g2, rwkv_k_k, rwkv_k_a, rwkv_r_k, rwkv_ln_w, rwkv_ln_b, w_branch, w_out, norm_post_mix, norm_pre_mlp, w_mlp_up, w_mlp_down, norm_post_mlp):
    xp = x_prompt.reshape(T_PROMPT, D_MODEL)
    xs = x_sample.reshape(B_SAMPLE * S_SAMPLE, D_MODEL)

    swap = np.concatenate([np.arange(ROPE // 2, ROPE), np.arange(ROPE // 2)])
    w = w_in[0].astype(bf16)
    mla_in = Q_LORA + KV_LORA + ROPE
    rwkv_in = 3 * D_MODEL + 4 * 96 + GATE_LORA
    w_kr = w[:, Q_LORA + KV_LORA:mla_in]
    w_krs = w_kr[:, swap]
    w_mla = jnp.concatenate([w[:, :Q_LORA + KV_LORA], w_kr, w_kr, w_krs, w_krs], axis=1)
    w_u = _pack_u(w[:, mla_in:mla_in + rwkv_in])
    w_g = w[:, mla_in + rwkv_in:]
    mup = _pack_u(mu_prev)
    mun = _pack_u(mu_next)

    wq3 = mla_w_uq[0].reshape(Q_LORA, MLA_HEADS, QK_DIM)
    wq_rope = wq3[:, :, NOPE:]
    w_q = jnp.concatenate([wq3[:, :, :NOPE].reshape(Q_LORA, -1),
                           wq_rope.reshape(Q_LORA, -1),
                           wq_rope[:, :, swap].reshape(Q_LORA, -1)], axis=1).astype(bf16)
    wkv3 = mla_w_ukv[0].reshape(KV_LORA, MLA_HEADS, NOPE + V_DIM)
    w_kv = jnp.concatenate([wkv3[:, :, :NOPE].reshape(KV_LORA, -1),
                            wkv3[:, :, NOPE:].reshape(KV_LORA, -1)], axis=1).astype(bf16)
    cos, sin = _rope_tables()

    qn, qr, kn, v, kr, h = _mla_front(xp, xs, norm_pre_mix, w_mla, mla_q_norm, w_q, mla_kv_norm, w_kv,
                                      cos, sin)
    ya_p = _attention(qn, qr, kn, kr, v, 0, T_PROMPT, 1, unroll=4, tq=TQ_ATTN_PROMPT)
    ya_s = _attention(qn, qr, kn, kr, v, T_PROMPT, S_SAMPLE, B_SAMPLE, unroll=2, tq=TQ_ATTN_SAMPLE)

    u = _proj_shift(h, w_u, 1.0 - mup - mun, mup, mun, "proj_rwkv")
    head_blk = np.arange(GROUP) // RWKV_HEAD
    bd = jnp.asarray(head_blk[:, None] == head_blk[None, :], dtype=bf16)
    vecs = (rwkv_w0_f, rwkv_w0_b, rwkv_a0_f, rwkv_a0_b, rwkv_k_k, rwkv_k_a,
            rwkv_r_k.reshape(1, D_MODEL))
    loras = tuple(_pad_rows(a[0]).astype(bf16) for a in (rwkv_w2_f, rwkv_w2_b, rwkv_a2_f, rwkv_a2_b))
    (r, vv, kk, k_f, k_b, b_f, b_b, lw_f, lw_b, g, bonus, gates) = _rwkv_prep(
        u, h, w_g, vecs, loras, rwkv_g2[0].astype(bf16), bd)
    y_f, y_b = _wkv_scan(r, vv, kk, k_f, b_f, lw_f, k_b, b_b, lw_b)

    wb = w_branch[0].astype(bf16)
    o = _branch(ya_p, ya_s, y_f, y_b, bonus, g, gates, rwkv_ln_w, rwkv_ln_b, bd,
                wb[:D_MODEL], wb[D_MODEL:])
    x1 = _out_proj(o, xp, xs, w_out[0].astype(bf16), norm_post_mix)
    out_p, out_s = _mlp(x1, norm_pre_mlp, w_mlp_up[0].astype(bf16), w_mlp_down[0].astype(bf16),
                        norm_post_mlp)
    return (out_p.reshape(1, T_PROMPT, D_MODEL), out_s.reshape(B_SAMPLE, S_SAMPLE, D_MODEL))
```

```python
import functools

import numpy as np
import jax
import jax.numpy as jnp
from jax import lax
from jax.experimental import pallas as pl
from jax.experimental.pallas import tpu as pltpu

f32 = jnp.float32
bf16 = jnp.bfloat16

D_MODEL = 2048
T_PROMPT = 8192
S_SAMPLE = 2048
B_SAMPLE = 4
T_ALL = T_PROMPT + B_SAMPLE * S_SAMPLE

MLA_HEADS = 16
Q_LORA = 768
KV_LORA = 512
NOPE = 128
ROPE = 64
V_DIM = 128
QK_DIM = NOPE + ROPE
ROPE_THETA = 10000.0
LOG2_E = 1.4426950408889634

RWKV_HEAD = 64
RWKV_HEADS = 32
LORA_PAD = 128
GATE_LORA = 256
U_PACKED = 3 * D_MODEL + 4 * LORA_PAD + GATE_LORA

D_FF = 4 * D_MODEL
NORM_EPS = 1e-6
GN_EPS = 64e-5

CHUNK = 64
GROUP = 256
N_GROUPS = D_MODEL // GROUP

VMEM_LIMIT = 56 * 1024 * 1024
TM_MLA_FRONT = 512
TM_PROJ, TN_PROJ = 1024, 768
TQ_ATTN_PROMPT, TQ_ATTN_SAMPLE, TK_ATTN = 512, 1024, 1024
TM_PREP = 128
WKV_CHUNKS, WKV_GROUPS = 4, 4
TM_BRANCH = 256
TM_OUT_PROJ = 512
TM_MLP, TF_MLP = 512, 1024


def _rms(x, g):
    return x * lax.rsqrt(jnp.mean(x * x, axis=-1, keepdims=True) + NORM_EPS) * g


def _dot(a, b):
    return jnp.dot(a, b, preferred_element_type=f32)


def _dot_nt(a, b):
    return lax.dot_general(a, b, (((1,), (1,)), ((), ())), preferred_element_type=f32)


def _split_dot(x, w):
    hi = x.astype(bf16)
    lo = (x - hi.astype(f32)).astype(bf16)
    return _dot(hi, w) + _dot(lo, w)


def _const_spec(shape):
    return pl.BlockSpec(shape, lambda *_: (0,) * len(shape), pipeline_mode=pl.Buffered(1))


def _x_specs(tm):
    npb = T_PROMPT // tm
    return [pl.BlockSpec((tm, D_MODEL), lambda i, *_: (jnp.minimum(i, npb - 1), 0)),
            pl.BlockSpec((tm, D_MODEL), lambda i, *_: (jnp.maximum(i - npb, 0), 0))]


def _is_sequence_edge(row):
    return jnp.logical_or(row == 0, jnp.logical_and(row >= T_PROMPT, (row - T_PROMPT) % S_SAMPLE == 0))


def _pick_x(xp_ref, xs_ref):
    tm = xp_ref.shape[0]
    return jnp.where(pl.program_id(0) < T_PROMPT // tm, xp_ref[...], xs_ref[...])


def _mla_front_kernel(xp_ref, xs_ref, g_ref, wm_ref, qg_ref, wq_ref, kvg_ref, wkv_ref, cos_ref, sin_ref,
                      qn_ref, qr_ref, kn_ref, v_ref, kr_ref, h_ref):
    h = _rms(_pick_x(xp_ref, xs_ref), g_ref[...]).astype(bf16)
    h_ref[...] = h
    pm = _dot(h, wm_ref[...])
    cos = cos_ref[...]
    sin = sin_ref[...]
    kr_ref[...] = (pm[:, 1280:1408] * cos + pm[:, 1408:1536] * sin).astype(bf16)
    qin = _rms(pm[:, :Q_LORA], qg_ref[...]).astype(bf16)
    kvin = _rms(pm[:, Q_LORA:Q_LORA + KV_LORA], kvg_ref[...]).astype(bf16)
    scale = QK_DIM ** -0.5 * LOG2_E
    cw = 512
    for c in range(D_MODEL // cw):
        q = _dot(qin, wq_ref[:, c * cw:(c + 1) * cw])
        qn_ref[:, c * cw:(c + 1) * cw] = (q * scale).astype(bf16)
    cos4 = jnp.concatenate([cos] * (cw // 128), axis=1)
    sin4 = jnp.concatenate([sin] * (cw // 128), axis=1)
    nrope = MLA_HEADS * ROPE
    for c in range(nrope // cw):
        qa = _dot(qin, wq_ref[:, D_MODEL + c * cw:D_MODEL + (c + 1) * cw])
        qb = _dot(qin, wq_ref[:, D_MODEL + nrope + c * cw:D_MODEL + nrope + (c + 1) * cw])
        qr_ref[:, c * cw:(c + 1) * cw] = ((qa * cos4 + qb * sin4) * scale).astype(bf16)
    for c in range(D_MODEL // cw):
        kn_ref[:, c * cw:(c + 1) * cw] = _dot(kvin, wkv_ref[:, c * cw:(c + 1) * cw]).astype(bf16)
        v_ref[:, c * cw:(c + 1) * cw] = _dot(
            kvin, wkv_ref[:, D_MODEL + c * cw:D_MODEL + (c + 1) * cw]).astype(bf16)


def _mla_front(xp, xs, g, w_mla, qg, w_q, kvg, w_kv, cos, sin, tm=TM_MLA_FRONT):
    nq = w_q.shape[1]
    row = lambda w: pl.BlockSpec((tm, w), lambda i: (i, 0))
    return pl.pallas_call(
        _mla_front_kernel,
        grid=(T_ALL // tm,),
        in_specs=_x_specs(tm) + [_const_spec((1, D_MODEL)), _const_spec(w_mla.shape),
                  _const_spec((1, Q_LORA)), _const_spec((Q_LORA, nq)),
                  _const_spec((1, KV_LORA)), _const_spec(w_kv.shape), row(128), row(128)],
        out_specs=[row(D_MODEL), row(MLA_HEADS * ROPE), row(D_MODEL), row(D_MODEL), row(128),
                   row(D_MODEL)],
        out_shape=[jax.ShapeDtypeStruct((T_ALL, D_MODEL), bf16),
                   jax.ShapeDtypeStruct((T_ALL, MLA_HEADS * ROPE), bf16),
                   jax.ShapeDtypeStruct((T_ALL, D_MODEL), bf16),
                   jax.ShapeDtypeStruct((T_ALL, D_MODEL), bf16),
                   jax.ShapeDtypeStruct((T_ALL, 128), bf16),
                   jax.ShapeDtypeStruct((T_ALL, D_MODEL), bf16)],
        compiler_params=pltpu.CompilerParams(dimension_semantics=("parallel",),
                                             vmem_limit_bytes=VMEM_LIMIT),
        name="mla_front",
    )(xp, xs, g, w_mla, qg, w_q, kvg, w_kv, cos, sin)


HALO = 16


def _proj_shift_kernel(hp_ref, h_ref, hn_ref, w_ref, mu0_ref, mup_ref, mun_ref, o_ref):
    tm = h_ref.shape[0]
    t0 = pl.program_id(0) * tm
    keep_prev = jnp.where(_is_sequence_edge(t0), 0.0, 1.0)
    keep_next = jnp.where(_is_sequence_edge(t0 + tm), 0.0, 1.0)
    lhs = jnp.concatenate([hp_ref[...], h_ref[...], hn_ref[...]], axis=0)
    ue = _dot(lhs, w_ref[...])
    n = tm + 2 * HALO
    down = pltpu.roll(ue, 1, axis=0)[HALO:HALO + tm]
    up = pltpu.roll(ue, n - 1, axis=0)[HALO:HALO + tm]
    row8 = lax.broadcasted_iota(jnp.int32, (8, 1), 0)
    head = jnp.where(row8 == 0, down[:8] * keep_prev, down[:8])
    tail = jnp.where(row8 == 7, up[tm - 8:] * keep_next, up[tm - 8:])
    prev = jnp.concatenate([head, down[8:]], axis=0)
    nxt = jnp.concatenate([up[:tm - 8], tail], axis=0)
    o_ref[...] = mu0_ref[...] * ue[HALO:HALO + tm] + mup_ref[...] * prev + mun_ref[...] * nxt


def _proj_shift(h, w, mu0, mup, mun, name, tm=TM_PROJ, tn=TN_PROJ):
    n = w.shape[1]
    hb = tm // HALO
    nblk = T_ALL // HALO
    col = pl.BlockSpec((1, tn), lambda i, j: (0, j))
    return pl.pallas_call(
        _proj_shift_kernel,
        grid=(T_ALL // tm, n // tn),
        in_specs=[pl.BlockSpec((HALO, D_MODEL), lambda i, j: (jnp.maximum(i * hb - 1, 0), 0)),
                  pl.BlockSpec((tm, D_MODEL), lambda i, j: (i, 0)),
                  pl.BlockSpec((HALO, D_MODEL), lambda i, j: (jnp.minimum((i + 1) * hb, nblk - 1), 0)),
                  pl.BlockSpec((D_MODEL, tn), lambda i, j: (0, j)), col, col, col],
        out_specs=pl.BlockSpec((tm, tn), lambda i, j: (i, j)),
        out_shape=jax.ShapeDtypeStruct((T_ALL, n), f32),
        compiler_params=pltpu.CompilerParams(dimension_semantics=("parallel", "arbitrary"),
                                             vmem_limit_bytes=VMEM_LIMIT),
        name=name,
    )(h, h, h, w, mu0, mup, mun)


def _attn_kernel(qn_ref, qr_ref, kn_ref, kr_ref, v_ref, o_ref, kcat_ref, vt_ref, s_ref, p_ref, *,
                 seq, tq, tk, unroll):
    head = pl.program_id(1)
    nsteps = seq // tk

    @pl.when(pl.program_id(2) == 0)
    def _():
        kcat_ref[:, :NOPE] = kn_ref[...]
        kcat_ref[:, NOPE:] = kr_ref[...]
        for c in range(seq // 512):
            cs = slice(c * 512, (c + 1) * 512)
            vt_ref[:, cs] = jnp.transpose(v_ref[cs, :].astype(f32)).astype(bf16)

    lane = lax.broadcasted_iota(jnp.int32, (tq, 128), 1)
    keep = (lane // ROPE) == (head % 2)
    qr = jnp.where(keep, qr_ref[...].astype(f32), 0.0).astype(bf16)
    q = jnp.concatenate([qn_ref[...], qr], axis=1)

    def scores(j, slot):
        rows = pl.ds(pl.multiple_of(j * tk, tk), tk)
        s_ref[slot] = _dot_nt(kcat_ref[rows, :], q)

    def absorb(j, slot, carry):
        m, l8, acc = carry
        s = s_ref[slot]
        m_loc = jnp.max(s.reshape(tk // 8, 8, tq), axis=0)
        m_new = jnp.maximum(m, jnp.max(m_loc, axis=0, keepdims=True))
        alpha = jnp.exp2(m - m_new)
        l8 = alpha * l8
        rc = 256
        for c in range(tk // rc):
            p = jnp.exp2(s_ref[slot, c * rc:(c + 1) * rc, :] - m_new)
            l8 = l8 + jnp.sum(p.reshape(rc // 8, 8, tq), axis=0)
            p_ref[c * rc:(c + 1) * rc, :] = p.astype(bf16)
        cols = pl.ds(pl.multiple_of(j * tk, tk), tk)
        acc = alpha * acc + _dot(vt_ref[:, cols], p_ref[...])
        return m_new, l8, acc

    def run(j0, carry, is_tail):
        for u in range(unroll):
            if not (is_tail and u == unroll - 1):
                scores(j0 + u + 1, (u + 1) % 2)
            carry = absorb(j0 + u, u % 2, carry)
        return carry

    carry = (jnp.full((1, tq), -jnp.inf, f32), jnp.zeros((8, tq), f32), jnp.zeros((V_DIM, tq), f32))
    scores(0, 0)
    carry = lax.fori_loop(0, nsteps // unroll - 1, lambda i, c: run(i * unroll, c, False), carry)
    _, l8, acc = run(nsteps - unroll, carry, True)
    out_t = acc / jnp.sum(l8, axis=0, keepdims=True)
    o_ref[...] = jnp.transpose(out_t).astype(o_ref.dtype)


def _attention(qn, qr, kn, kr, v, row_off, seq, nb, unroll, tq, tk=TK_ATTN):
    qoff = row_off // tq
    soff = row_off // seq
    nq = seq // tq
    return pl.pallas_call(
        functools.partial(_attn_kernel, seq=seq, tq=tq, tk=tk, unroll=unroll),
        grid=(nb, MLA_HEADS, nq),
        in_specs=[pl.BlockSpec((tq, NOPE), lambda b, h, i: (qoff + b * nq + i, h)),
                  pl.BlockSpec((tq, 128), lambda b, h, i: (qoff + b * nq + i, h // 2)),
                  pl.BlockSpec((seq, NOPE), lambda b, h, i: (soff + b, h)),
                  pl.BlockSpec((seq, 128), lambda b, h, i: (soff + b, 0)),
                  pl.BlockSpec((seq, V_DIM), lambda b, h, i: (soff + b, h))],
        out_specs=pl.BlockSpec((tq, V_DIM), lambda b, h, i: (b * nq + i, h)),
        out_shape=jax.ShapeDtypeStruct((nb * seq, D_MODEL), bf16),
        scratch_shapes=[pltpu.VMEM((seq, 2 * NOPE), bf16), pltpu.VMEM((V_DIM, seq), bf16),
                        pltpu.VMEM((2, tk, tq), f32), pltpu.VMEM((tk, tq), bf16)],
        compiler_params=pltpu.CompilerParams(
            dimension_semantics=("parallel", "parallel", "arbitrary"),
            vmem_limit_bytes=VMEM_LIMIT),
        name=f"mla_attention_s{seq}",
    )(qn, qr, kn, kr, v)


def _seg_sum(x, bd, split=True):
    cols = []
    for c in range(x.shape[1] // GROUP):
        xc = x[:, c * GROUP:(c + 1) * GROUP]
        cols.append(_split_dot(xc, bd) if split else _dot(xc.astype(bf16), bd))
    return jnp.concatenate(cols, axis=1)


def _store_groups(ref, x):
    for g in range(N_GROUPS):
        ref[g] = x[:, g * GROUP:(g + 1) * GROUP]


def _sigmoid(z):
    return 0.5 * jnp.tanh(0.5 * z) + 0.5


def _rwkv_prep_kernel(u_ref, h_ref, wg_ref,
                      w0f_ref, w2f_ref, w0b_ref, w2b_ref, a0f_ref, a2f_ref, a0b_ref, a2b_ref,
                      g2_ref, kkw_ref, kaw_ref, rk_ref, bd_ref,
                      r_ref, v_ref, kk_ref, kf_ref, kb_ref, bf_ref, bb_ref, lwf_ref, lwb_ref,
                      g_ref, bonus_ref, gates_ref):
    cw = 512
    h = h_ref[...]
    for c in range(2 * D_MODEL // cw):
        cs = slice(c * cw, (c + 1) * cw)
        gates_ref[:, cs] = _sigmoid(_dot(h, wg_ref[:, cs])).astype(bf16)

    d = D_MODEL
    r = u_ref[:, 0:d]
    k = u_ref[:, d:2 * d]
    v = u_ref[:, 2 * d:3 * d]
    xw_f = u_ref[:, 3 * d:3 * d + 128]
    xw_b = u_ref[:, 3 * d + 128:3 * d + 256]
    xa_f = u_ref[:, 3 * d + 256:3 * d + 384]
    xa_b = u_ref[:, 3 * d + 384:3 * d + 512]
    xg = u_ref[:, 3 * d + 512:U_PACKED]

    def neg_exp_logw(xw, w0_ref, w2_ref):
        z = w0_ref[...] + _dot(jnp.tanh(xw).astype(bf16), w2_ref[...])
        return (-np.exp(-0.5)) * _sigmoid(z)

    _store_groups(lwf_ref, neg_exp_logw(xw_f, w0f_ref, w2f_ref))
    _store_groups(lwb_ref, neg_exp_logw(xw_b, w0b_ref, w2b_ref))
    a_f = _sigmoid(a0f_ref[...] + _dot(xa_f.astype(bf16), a2f_ref[...]))
    a_b = _sigmoid(a0b_ref[...] + _dot(xa_b.astype(bf16), a2b_ref[...]))
    g_ref[...] = _dot(_sigmoid(xg).astype(bf16), g2_ref[...]).astype(bf16)

    bd = bd_ref[...]
    kk = k * kkw_ref[...]
    kk = kk * lax.rsqrt(jnp.maximum(_seg_sum(kk * kk, bd, split=False), 1e-24))
    ka = kaw_ref[...]
    k_f = k * (1.0 + (a_f - 1.0) * ka)
    k_b = k * (1.0 + (a_b - 1.0) * ka)
    bonus = _seg_sum(r * (k_f + k_b) * rk_ref[...], bd, split=False) * v

    _store_groups(r_ref, r.astype(bf16))
    _store_groups(v_ref, v.astype(bf16))
    _store_groups(kk_ref, kk.astype(bf16))
    _store_groups(kf_ref, k_f.astype(bf16))
    _store_groups(kb_ref, k_b.astype(bf16))
    _store_groups(bf_ref, (kk * a_f).astype(bf16))
    _store_groups(bb_ref, (kk * a_b).astype(bf16))
    bonus_ref[...] = bonus.astype(bf16)


def _rwkv_prep(u, h, w_g, vecs, loras, g2, bd, tm=TM_PREP):
    vec = _const_spec((1, D_MODEL))
    lora = _const_spec((LORA_PAD, D_MODEL))
    w0f, w0b, a0f, a0b, kkw, kaw, rk = vecs
    w2f, w2b, a2f, a2b = loras
    out_bf = jax.ShapeDtypeStruct((T_ALL, D_MODEL), bf16)
    grp_bf = jax.ShapeDtypeStruct((N_GROUPS, T_ALL, GROUP), bf16)
    grp_f = jax.ShapeDtypeStruct((N_GROUPS, T_ALL, GROUP), f32)
    row = pl.BlockSpec((tm, D_MODEL), lambda i: (i, 0))
    grp = pl.BlockSpec((N_GROUPS, tm, GROUP), lambda i: (0, i, 0))
    return pl.pallas_call(
        _rwkv_prep_kernel,
        grid=(T_ALL // tm,),
        in_specs=[pl.BlockSpec((tm, U_PACKED), lambda i: (i, 0)), row, _const_spec(w_g.shape),
                  vec, lora, vec, lora, vec, lora, vec, lora,
                  _const_spec((GATE_LORA, D_MODEL)), vec, vec, vec, _const_spec((GROUP, GROUP))],
        out_specs=[grp] * 9 + [row, row, pl.BlockSpec((tm, 2 * D_MODEL), lambda i: (i, 0))],
        out_shape=[grp_bf] * 7 + [grp_f, grp_f, out_bf, out_bf,
                                  jax.ShapeDtypeStruct((T_ALL, 2 * D_MODEL), bf16)],
        compiler_params=pltpu.CompilerParams(dimension_semantics=("parallel",),
                                             vmem_limit_bytes=VMEM_LIMIT),
        name="rwkv_prep_gates",
    )(u, h, w_g, w0f, w2f, w0b, w2b, a0f, a2f, a0b, a2b, g2, kkw, kaw, rk, bd)


def _wkv_masks(reverse):
    C = CHUNK
    t_i = lax.broadcasted_iota(jnp.int32, (C, GROUP), 0)
    s_i = lax.broadcasted_iota(jnp.int32, (C, GROUP), 1) % C
    strict = (s_i > t_i) if reverse else (s_i < t_i)
    incl = (s_i >= t_i) if reverse else (s_i <= t_i)
    eye = jnp.where(s_i == t_i, 1.0, 0.0)
    tt = lax.broadcasted_iota(jnp.int32, (C, C), 0)
    ss = lax.broadcasted_iota(jnp.int32, (C, C), 1)
    tri = jnp.where((ss >= tt) if reverse else (ss <= tt), 1.0, 0.0).astype(bf16)
    return strict, incl, eye, tri


def _wkv_chunk_operators(chains, bdmask):
    C = CHUNK
    n = range(len(chains))
    rs, ks, vs, kks, bs, lws, masks, revs = zip(*chains)
    strict = [m[0] for m in masks]
    incl = [m[1] for m in masks]
    eye = [m[2] for m in masks]
    tri = [m[3] for m in masks]
    last = [0 if rev else C - 1 for rev in revs]
    even_blk = (lax.broadcasted_iota(jnp.int32, (2 * C, GROUP), 1) // RWKV_HEAD) % 2 == 0

    def bd(x):
        return jnp.where(bdmask, jnp.concatenate([x] * 4, axis=0), 0.0).astype(bf16)

    def ss_mul(x, y_bd):
        return _dot(x.astype(bf16), y_bd)

    lw_hi = [lws[i].astype(bf16) for i in n]
    lw_lo = [(lws[i] - lw_hi[i].astype(f32)).astype(bf16) for i in n]
    cum = [_dot(tri[i], lw_hi[i]) + _dot(tri[i], lw_lo[i]) for i in n]
    w_inv = [jnp.exp(-cum[i]) for i in n]
    w_rem = [jnp.exp(cum[i][last[i]:last[i] + 1, :] - cum[i]) for i in n]
    a_t = [-kks[i] * jnp.exp(cum[i] - lws[i]) for i in n]
    r_t = [rs[i] * jnp.exp(cum[i]) for i in n]

    lhs = [jnp.concatenate([a_t[i], r_t[i]], axis=0).astype(bf16) for i in n]
    b_i = [bs[i] * w_inv[i] for i in n]
    k_i = [ks[i] * w_inv[i] for i in n]
    bk_t = [jnp.transpose(jnp.concatenate([b_i[i], k_i[i]], axis=0)) for i in n]
    kb_t = [jnp.transpose(jnp.concatenate([k_i[i], b_i[i]], axis=0)) for i in n]
    mix1 = [_dot(lhs[i], jnp.where(bdmask, jnp.concatenate([bk_t[i]] * 2, axis=1), 0.0).astype(bf16))
            for i in n]
    mix2 = [_dot(lhs[i], jnp.where(bdmask, jnp.concatenate([kb_t[i]] * 2, axis=1), 0.0).astype(bf16))
            for i in n]
    ab = [jnp.where(even_blk, mix1[i], mix2[i]) for i in n]
    ak = [jnp.where(even_blk, mix2[i], mix1[i]) for i in n]
    a_ab = [jnp.where(strict[i], ab[i][:C], 0.0) for i in n]
    a_rb = [jnp.where(incl[i], ab[i][C:], 0.0) for i in n]
    a_k = [jnp.concatenate([jnp.where(strict[i], ak[i][:C], 0.0),
                            jnp.where(incl[i], ak[i][C:], 0.0)], axis=0) for i in n]

    tinv = [eye[i] + a_ab[i] for i in n]
    pw = [ss_mul(a_ab[i], bd(a_ab[i])) for i in n]
    for _ in range(4):
        both = [ss_mul(jnp.concatenate([tinv[i], pw[i]], axis=0), bd(pw[i])) for i in n]
        tinv = [tinv[i] + both[i][:C] for i in n]
        pw = [both[i][C:] for i in n]
    tinv = [tinv[i] + ss_mul(tinv[i], bd(pw[i])) for i in n]

    zr = [ss_mul(a_k[i], bd(vs[i])) for i in n]
    art = [ss_mul(a_rb[i], bd(tinv[i])) for i in n]
    pqy = [ss_mul(jnp.concatenate([tinv[i], art[i]], axis=0),
                  jnp.concatenate([bd(a_t[i]), bd(zr[i][:C])], axis=1)) for i in n]
    p = [pqy[i][:C, :GROUP] for i in n]
    q = [pqy[i][:C, GROUP:] for i in n]
    y_op = [r_t[i] + pqy[i][C:, :GROUP] for i in n]
    y_c = [pqy[i][C:, GROUP:] + zr[i][C:] for i in n]
    py = [jnp.concatenate([p[i], y_op[i]], axis=0).astype(bf16) for i in n]
    bkt = [jnp.transpose(jnp.concatenate([bs[i] * w_rem[i], ks[i] * w_rem[i]], axis=0)).astype(bf16)
           for i in n]
    w_tot = [jnp.broadcast_to(jnp.exp(jnp.transpose(cum[i])[:, last[i]:last[i] + 1]), (GROUP, 128))
             for i in n]
    return [(py[i], q[i], y_c[i], bkt[i], w_tot[i]) for i in n]


def _wkv_kernel(rf_ref, vf_ref, kkf_ref, kf_ref, bf_ref, lwf_ref,
                rb_ref, vb_ref, kkb_ref, kb_ref, bb_ref, lwb_ref,
                yf_ref, yb_ref,
                h_ref, py_ref, q_ref, yc_ref, bkt_ref, wt_ref, *, cb, ng):
    C = CHUNK

    c = pl.program_id(1)
    for d, edge in enumerate((c * (cb * C), (pl.num_programs(1) - c) * (cb * C))):
        @pl.when(_is_sequence_edge(edge))
        def _():
            for g in range(ng):
                h_ref[g, d] = jnp.zeros((GROUP, GROUP), f32)

    br = lax.broadcasted_iota(jnp.int32, (GROUP, GROUP), 0) // RWKV_HEAD
    bc = lax.broadcasted_iota(jnp.int32, (GROUP, GROUP), 1) // RWKV_HEAD
    bdmask = br == bc
    dirs = ((rf_ref, vf_ref, kkf_ref, kf_ref, bf_ref, lwf_ref, yf_ref, _wkv_masks(False)),
            (rb_ref, vb_ref, kkb_ref, kb_ref, bb_ref, lwb_ref, yb_ref, _wkv_masks(True)))

    def operators(gp, carry):
        chains, slots = [], []
        for g in (4 * gp, 4 * gp + 1, 4 * gp + 2, 4 * gp + 3):
            for j in range(cb):
                rows = pl.ds(j * C, C)
                for d, (r_ref, v_ref, kk_ref, k_ref, b_ref, lw_ref, _, masks) in enumerate(dirs):
                    chains.append((r_ref[g, rows, :].astype(f32), k_ref[g, rows, :].astype(f32),
                                   v_ref[g, rows, :].astype(f32), kk_ref[g, rows, :].astype(f32),
                                   b_ref[g, rows, :].astype(f32), lw_ref[g, rows, :], masks, d == 1))
                    slots.append((g, d, j))
        for (g, d, j), (py, q, y_c, bkt, wt) in zip(slots, _wkv_chunk_operators(chains, bdmask)):
            py_ref[g, d, j] = py
            q_ref[g, d, j] = q
            yc_ref[g, d, j] = y_c
            bkt_ref[g, d, j] = bkt
            wt_ref[g, d, j] = wt
        return carry

    lax.fori_loop(0, ng // 4, operators, 0)

    def recur(j, carry):
        jd = (j, cb - 1 - j)
        rows = [pl.ds(pl.multiple_of(jd[d] * C, C), C) for d in range(2)]
        gd = [(g, d) for g in range(ng) for d in range(2)]
        h = [h_ref[g, d] for g, d in gd]
        uy = [_dot(py_ref[g, d, jd[d]], h[i].astype(bf16)) for i, (g, d) in enumerate(gd)]
        uv = []
        for i, (g, d) in enumerate(gd):
            dirs[d][6][g, rows[d], :] = uy[i][C:] + yc_ref[g, d, jd[d]]
            u = uy[i][:C] + q_ref[g, d, jd[d]]
            uv.append(jnp.concatenate([u.astype(bf16), dirs[d][1][g, rows[d], :]], axis=0))
        upd = [_dot(bkt_ref[g, d, jd[d]], uv[i]) for i, (g, d) in enumerate(gd)]
        for i, (g, d) in enumerate(gd):
            wt = wt_ref[g, d, jd[d]]
            h_ref[g, d] = h[i] * jnp.concatenate([wt, wt], axis=1) + jnp.where(bdmask, upd[i], 0.0)
        return carry

    lax.fori_loop(0, cb, recur, 0)


def _wkv_scan(r, v, kk, k_f, b_f, lw_f, k_b, b_b, lw_b, cb=WKV_CHUNKS, ng=WKV_GROUPS):
    C = CHUNK
    rb = cb * C
    nblk = T_ALL // rb
    fwd = pl.BlockSpec((ng, rb, GROUP), lambda g, c: (g, c, 0))
    bwd = pl.BlockSpec((ng, rb, GROUP), lambda g, c: (g, nblk - 1 - c, 0))
    out = jax.ShapeDtypeStruct((N_GROUPS, T_ALL, GROUP), f32)
    return pl.pallas_call(
        functools.partial(_wkv_kernel, cb=cb, ng=ng),
        grid=(N_GROUPS // ng, nblk),
        in_specs=[fwd] * 6 + [bwd] * 6,
        out_specs=[fwd, bwd],
        out_shape=[out, out],
        scratch_shapes=[pltpu.VMEM((ng, 2, GROUP, GROUP), f32),
                        pltpu.VMEM((ng, 2, cb, 2 * C, GROUP), bf16),
                        pltpu.VMEM((ng, 2, cb, C, GROUP), f32),
                        pltpu.VMEM((ng, 2, cb, C, GROUP), f32),
                        pltpu.VMEM((ng, 2, cb, GROUP, 2 * C), bf16),
                        pltpu.VMEM((ng, 2, cb, GROUP, 128), f32)],
        compiler_params=pltpu.CompilerParams(
            dimension_semantics=("parallel", "arbitrary"),
            vmem_limit_bytes=VMEM_LIMIT),
        name="wkv_scan",
    )(r, v, kk, k_f, b_f, lw_f, r, v, kk, k_b, b_b, lw_b)


def _branch_kernel(yap_ref, yas_ref, yf_ref, yb_ref, bonus_ref, g_ref, g1_ref, g2_ref, lnw_ref, lnb_ref,
                   bd_ref, wa_ref, wb_ref, o_ref):
    bd = bd_ref[...]
    y = jnp.concatenate([yf_ref[g] + yb_ref[g] for g in range(N_GROUPS)], axis=1)
    inv_n = 1.0 / RWKV_HEAD
    mean = _seg_sum(y, bd) * inv_n
    dlt = y - mean
    var = _seg_sum(dlt * dlt, bd) * inv_n
    yn = dlt * lax.rsqrt(var + GN_EPS) * lnw_ref[...] + lnb_ref[...]
    yb = ((yn + bonus_ref[...].astype(f32)) * g_ref[...].astype(f32)).astype(bf16)
    ya = _pick_x(yap_ref, yas_ref)
    cw = 512
    for c in range(D_MODEL // cw):
        cs = slice(c * cw, (c + 1) * cw)
        oa = _dot(ya, wa_ref[:, cs])
        ob = _dot(yb, wb_ref[:, cs])
        o_ref[:, cs] = (g1_ref[:, cs].astype(f32) * oa + g2_ref[:, cs].astype(f32) * ob).astype(bf16)


def _branch(ya_p, ya_s, yf, yb, bonus, g, gates, lnw, lnb, bd, wa, wb, tm=TM_BRANCH):
    row = pl.BlockSpec((tm, D_MODEL), lambda i: (i, 0))
    grp = pl.BlockSpec((N_GROUPS, tm, GROUP), lambda i: (0, i, 0))
    return pl.pallas_call(
        _branch_kernel,
        grid=(T_ALL // tm,),
        in_specs=_x_specs(tm) + [grp, grp, row, row,
                  pl.BlockSpec((tm, D_MODEL), lambda i: (i, 0)),
                  pl.BlockSpec((tm, D_MODEL), lambda i: (i, 1)),
                  _const_spec((1, D_MODEL)), _const_spec((1, D_MODEL)), _const_spec((GROUP, GROUP)),
                  _const_spec((D_MODEL, D_MODEL)), _const_spec((D_MODEL, D_MODEL))],
        out_specs=row,
        out_shape=jax.ShapeDtypeStruct((T_ALL, D_MODEL), bf16),
        compiler_params=pltpu.CompilerParams(dimension_semantics=("parallel",),
                                             vmem_limit_bytes=VMEM_LIMIT),
        name="branch_mix",
    )(ya_p, ya_s, yf, yb, bonus, g, gates, gates, lnw, lnb, bd, wa, wb)


def _out_proj_kernel(o_ref, xp_ref, xs_ref, w_ref, g_ref, x1_ref):
    mix = _dot(o_ref[...], w_ref[...])
    x1_ref[...] = _pick_x(xp_ref, xs_ref) + _rms(mix, g_ref[...])


def _out_proj(o, xp, xs, w, g, tm=TM_OUT_PROJ):
    row = pl.BlockSpec((tm, D_MODEL), lambda i: (i, 0))
    return pl.pallas_call(
        _out_proj_kernel,
        grid=(T_ALL // tm,),
        in_specs=[row] + _x_specs(tm) + [_const_spec((D_MODEL, D_MODEL)), _const_spec((1, D_MODEL))],
        out_specs=row,
        out_shape=jax.ShapeDtypeStruct((T_ALL, D_MODEL), f32),
        compiler_params=pltpu.CompilerParams(dimension_semantics=("parallel",),
                                             vmem_limit_bytes=VMEM_LIMIT),
        name="out_proj",
    )(o, xp, xs, w, g)


def _mlp_kernel(x_ref, gpre_ref, wu_ref, wd_ref, gpost_ref, op_ref, os_ref, xn_ref, acc_ref):
    i = pl.program_id(0)
    j = pl.program_id(1)
    n_prompt_tiles = T_PROMPT // x_ref.shape[0]

    @pl.when(j == 0)
    def _():
        xn_ref[...] = _rms(x_ref[...], gpre_ref[...]).astype(bf16)
        acc_ref[...] = jnp.zeros_like(acc_ref)

    h = jnp.maximum(_dot(xn_ref[...], wu_ref[...]), 0.0)
    acc_ref[...] += _dot((h * h).astype(bf16), wd_ref[...])

    last = j == pl.num_programs(1) - 1

    @pl.when(jnp.logical_and(last, i < n_prompt_tiles))
    def _():
        op_ref[...] = x_ref[...] + _rms(acc_ref[...], gpost_ref[...])

    @pl.when(jnp.logical_and(last, i >= n_prompt_tiles))
    def _():
        os_ref[...] = x_ref[...] + _rms(acc_ref[...], gpost_ref[...])


def _mlp(x1, gpre, w_up, w_down, gpost, tm=TM_MLP, tf=TF_MLP):
    npb = T_PROMPT // tm
    row = pl.BlockSpec((tm, D_MODEL), lambda i, j: (i, 0))
    out = jax.ShapeDtypeStruct((T_PROMPT, D_MODEL), f32)
    return pl.pallas_call(
        _mlp_kernel,
        grid=(T_ALL // tm, D_FF // tf),
        in_specs=[row, _const_spec((1, D_MODEL)),
                  pl.BlockSpec((D_MODEL, tf), lambda i, j: (0, j)),
                  pl.BlockSpec((tf, D_MODEL), lambda i, j: (j, 0)),
                  _const_spec((1, D_MODEL))],
        out_specs=[pl.BlockSpec((tm, D_MODEL), lambda i, j: (jnp.minimum(i, npb - 1), 0)),
                   pl.BlockSpec((tm, D_MODEL), lambda i, j: (jnp.maximum(i - npb, 0), 0))],
        out_shape=[out, out],
        scratch_shapes=[pltpu.VMEM((tm, D_MODEL), bf16), pltpu.VMEM((tm, D_MODEL), f32)],
        compiler_params=pltpu.CompilerParams(dimension_semantics=("arbitrary", "arbitrary"),
                                             vmem_limit_bytes=VMEM_LIMIT),
        name="mlp",
    )(x1, gpre, w_up, w_down, gpost)


def _pad_lanes(a, width):
    return jnp.pad(a, [(0, 0)] * (a.ndim - 1) + [(0, width - a.shape[-1])])


def _pack_u(a):
    d = D_MODEL
    cuts = [3 * d, 3 * d + 96, 3 * d + 192, 3 * d + 288, 3 * d + 384]
    return jnp.concatenate(
        [a[..., :cuts[0]]]
        + [_pad_lanes(a[..., cuts[i]:cuts[i + 1]], LORA_PAD) for i in range(4)]
        + [a[..., cuts[4]:]], axis=-1)


def _pad_rows(w):
    return jnp.pad(w, [(0, LORA_PAD - w.shape[0]), (0, 0)])


def _rope_tables():
    inv = 1.0 / (ROPE_THETA ** (jnp.arange(0, ROPE, 2, dtype=f32) / ROPE))
    pos = jnp.concatenate([jnp.arange(T_PROMPT, dtype=f32),
                           jnp.tile(jnp.arange(S_SAMPLE, dtype=f32), B_SAMPLE)])
    ang = pos[:, None] * inv[None, :]
    cos, sin = jnp.cos(ang), jnp.sin(ang)
    cos = jnp.concatenate([cos, cos] * 2, axis=1)
    sin = jnp.concatenate([-sin, sin] * 2, axis=1)
    return cos, sin


def kernel(x_prompt, x_sample, norm_pre_mix, w_in, mu_prev, mu_next, mla_q_norm, mla_w_uq, mla_kv_norm, mla_w_ukv, rwkv_w0_f, rwkv_w2_f, rwkv_w0_b, rwkv_w2_b, rwkv_a0_f, rwkv_a2_f, rwkv_a0_b, rwkv_a2_b, rwkv_g2, rwkv_k_k, rwkv_k_a, rwkv_r_k, rwkv_ln_w, rwkv_ln_b, w_branch, w_out, norm_post_mix, norm_pre_mlp, w_mlp_up, w_mlp_down, norm_post_mlp):
    xp = x_prompt.reshape(T_PROMPT, D_MODEL)
    xs = x_sample.reshape(B_SAMPLE * S_SAMPLE, D_MODEL)

    swap = np.concatenate([np.arange(ROPE // 2, ROPE), np.arange(ROPE // 2)])
    w = w_in[0].astype(bf16)
    mla_in = Q_LORA + KV_LORA + ROPE
    rwkv_in = 3 * D_MODEL + 4 * 96 + GATE_LORA
    w_kr = w[:, Q_LORA + KV_LORA:mla_in]
    w_krs = w_kr[:, swap]
    w_mla = jnp.concatenate([w[:, :Q_LORA + KV_LORA], w_kr, w_kr, w_krs, w_krs], axis=1)
    w_u = _pack_u(w[:, mla_in:mla_in + rwkv_in])
    w_g = w[:, mla_in + rwkv_in:]
    mup = _pack_u(mu_prev)
    mun = _pack_u(mu_next)

    wq3 = mla_w_uq[0].reshape(Q_LORA, MLA_HEADS, QK_DIM)
    wq_rope = wq3[:, :, NOPE:]
    w_q = jnp.concatenate([wq3[:, :, :NOPE].reshape(Q_LORA, -1),
                           wq_rope.reshape(Q_LORA, -1),
                           wq_rope[:, :, swap].reshape(Q_LORA, -1)], axis=1).astype(bf16)
    wkv3 = mla_w_ukv[0].reshape(KV_LORA, MLA_HEADS, NOPE + V_DIM)
    w_kv = jnp.concatenate([wkv3[:, :, :NOPE].reshape(KV_LORA, -1),
                            wkv3[:, :, NOPE:].reshape(KV_LORA, -1)], axis=1).astype(bf16)
    cos, sin = _rope_tables()

    qn, qr, kn, v, kr, h = _mla_front(xp, xs, norm_pre_mix, w_mla, mla_q_norm, w_q, mla_kv_norm, w_kv,
                                      cos, sin)
    ya_p = _attention(qn, qr, kn, kr, v, 0, T_PROMPT, 1, unroll=4, tq=TQ_ATTN_PROMPT)
    ya_s = _attention(qn, qr, kn, kr, v, T_PROMPT, S_SAMPLE, B_SAMPLE, unroll=2, tq=TQ_ATTN_SAMPLE)

    u = _proj_shift(h, w_u, 1.0 - mup - mun, mup, mun, "proj_rwkv")
    head_blk = np.arange(GROUP) // RWKV_HEAD
    bd = jnp.asarray(head_blk[:, None] == head_blk[None, :], dtype=bf16)
    vecs = (rwkv_w0_f, rwkv_w0_b, rwkv_a0_f, rwkv_a0_b, rwkv_k_k, rwkv_k_a,
            rwkv_r_k.reshape(1, D_MODEL))
    loras = tuple(_pad_rows(a[0]).astype(bf16) for a in (rwkv_w2_f, rwkv_w2_b, rwkv_a2_f, rwkv_a2_b))
    (r, vv, kk, k_f, k_b, b_f, b_b, lw_f, lw_b, g, bonus, gates) = _rwkv_prep(
        u, h, w_g, vecs, loras, rwkv_g2[0].astype(bf16), bd)
    y_f, y_b = _wkv_scan(r, vv, kk, k_f, b_f, lw_f, k_b, b_b, lw_b)

    wb = w_branch[0].astype(bf16)
    o = _branch(ya_p, ya_s, y_f, y_b, bonus, g, gates, rwkv_ln_w, rwkv_ln_b, bd,
                wb[:D_MODEL], wb[D_MODEL:])
    x1 = _out_proj(o, xp, xs, w_out[0].astype(bf16), norm_post_mix)
    out_p, out_s = _mlp(x1, norm_pre_mlp, w_mlp_up[0].astype(bf16), w_mlp_down[0].astype(bf16),
                        norm_post_mlp)
    return (out_p.reshape(1, T_PROMPT, D_MODEL), out_s.reshape(B_SAMPLE, S_SAMPLE, D_MODEL))
```
